```python
import math
import jax, jax.numpy as jnp
from jax import lax
import numpy as np

D_MODEL = 1024
BATCH = 8
SEQ = 2048
DEPTH = 4
DEC_BATCH = 128
DEC_SEQ = 1
PAST_LEN = 16384
PAGE_SIZE = 128

MIX_DIM = D_MODEL
DN_DIM = MIX_DIM // 2
POOL_DIM = MIX_DIM - DN_DIM
DN_HEADS = 4
DN_HEAD_DIM = DN_DIM // DN_HEADS
QKV_DIM = 3 * DN_DIM
CONV_W = 4
CHUNK = 64
POOL_WINDOWS = (2, 4, 8, 16)
POOL_GROUPS = len(POOL_WINDOWS)
POOL_GROUP_DIM = POOL_DIM // POOL_GROUPS
POOL_BUF = max(POOL_WINDOWS) - 1
D_FF = 2816
IN_DIM = QKV_DIM + DN_DIM + 2 * DN_HEADS + POOL_DIM
DN_ALPHA = (2.0 * DEPTH) ** 0.25
DN_BETA = (8.0 * DEPTH) ** -0.25
LN_EPS = 1e-5
RMS_EPS = 1e-6
L2_EPS = 1e-6

kernel_name = "hymba_gdn_pool_macaron_deepnorm_step"


def layer_norm(x, g, b):
    xf = x.astype(jnp.float32)
    mu = jnp.mean(xf, axis=-1, keepdims=True)
    xc = xf - mu
    var = jnp.mean(xc * xc, axis=-1, keepdims=True)
    return (xc * lax.rsqrt(var + LN_EPS) * g.astype(jnp.float32) + b.astype(jnp.float32)).astype(x.dtype)


def swiglu(x, wg, wu, wd):
    return (jax.nn.silu(x @ wg) * (x @ wu)) @ wd


def l2norm(x):
    return x * lax.rsqrt(jnp.sum(x * x, axis=-1, keepdims=True) + L2_EPS)


def causal_conv(xin, buf, w):
    L = xin.shape[1]
    ext = jnp.concatenate([buf.astype(xin.dtype), xin], axis=1)
    out = sum(w[i] * ext[:, i:i + L] for i in range(CONV_W))
    return jax.nn.silu(out), ext[:, -(CONV_W - 1):]


def gated_delta_chunked(q, k, v, beta, g, s0, chunk):
    B, L, H, DK = q.shape
    DV = v.shape[-1]
    N = L // chunk

    def blk(t):
        t = t.reshape((B, N, chunk, H) + t.shape[3:])
        return jnp.moveaxis(t, (1, 3), (0, 2))

    qb, kb, vb, bb, gb = blk(q), blk(k), blk(v), blk(beta), blk(g)
    gc = jnp.cumsum(gb, axis=-1)
    idx = jnp.arange(chunk)
    incl = idx[:, None] >= idx[None, :]
    strict = idx[:, None] > idx[None, :]
    diff = gc[..., :, None] - gc[..., None, :]
    dec = jnp.where(incl, jnp.exp(jnp.where(incl, diff, 0.0)), 0.0)
    kk = jnp.einsum('nbhcd,nbhsd->nbhcs', kb, kb)
    m = jnp.where(strict, bb[..., :, None] * kk * dec, 0.0)
    a_mat = jnp.eye(chunk, dtype=jnp.float32) + m
    rhs = jnp.concatenate([bb[..., None] * vb, (bb * jnp.exp(gc))[..., None] * kb], axis=-1)
    sol = lax.linalg.triangular_solve(a_mat, rhs, left_side=True, lower=True, unit_diagonal=True)
    uv, wk = sol[..., :DV], sol[..., DV:]
    qk = jnp.einsum('nbhcd,nbhsd->nbhcs', qb, kb) * dec
    q_dec = qb * jnp.exp(gc)[..., None]
    k_end = kb * jnp.exp(gc[..., -1:] - gc)[..., None]
    g_end = jnp.exp(gc[..., -1])

    def step(s, xs):
        uv_c, w_c, qk_c, qd_c, ke_c, ge_c = xs
        u = uv_c - jnp.einsum('bhcd,bhde->bhce', w_c, s)
        o = jnp.einsum('bhcd,bhde->bhce', qd_c, s) + jnp.einsum('bhcs,bhse->bhce', qk_c, u)
        s = ge_c[..., None, None] * s + jnp.einsum('bhcd,bhce->bhde', ke_c, u)
        return s, o

    s_fin, o = lax.scan(step, s0, (uv, wk, qk, q_dec, k_end, g_end))
    o = jnp.moveaxis(o, (0, 2), (1, 3)).reshape(B, L, H, DV)
    return o, s_fin


def pool_mix(p, buf, start_pos, pool_w, pool_scale):
    B, L, _ = p.shape
    ext = jnp.concatenate([buf.astype(jnp.float32), p.astype(jnp.float32)], axis=1)
    cs = jnp.concatenate([jnp.zeros((B, 1, POOL_DIM), jnp.float32), jnp.cumsum(ext, axis=1)], axis=1)
    end = cs[:, POOL_BUF + 1:]
    pos = start_pos + jnp.arange(L)
    means = []
    for gi, w in enumerate(POOL_WINDOWS):
        sl = slice(gi * POOL_GROUP_DIM, (gi + 1) * POOL_GROUP_DIM)
        s = end[:, :, sl] - cs[:, POOL_BUF + 1 - w:POOL_BUF + 1 - w + L, sl]
        cnt = jnp.minimum(w, pos + 1).astype(jnp.float32)
        means.append(s / cnt[None, :, None])
    d = jnp.concatenate(means, axis=-1) - ext[:, POOL_BUF:]
    d = d.reshape(B, L, POOL_GROUPS, POOL_GROUP_DIM).astype(p.dtype)
    y = jnp.einsum('blgc,gcd->blgd', d, pool_w).reshape(B, L, POOL_DIM) * pool_scale
    return y, ext[:, -POOL_BUF:].astype(p.dtype)


def mixer(h, s0, conv_buf, pool_buf, start_pos, chunk, w_in, conv_w, a_log, dt_bias, onorm_g, pool_w, pool_scale, w_out):
    B, L, _ = h.shape
    proj = h @ w_in
    qkv, z, b_raw, a_raw, p = jnp.split(
        proj, [QKV_DIM, QKV_DIM + DN_DIM, QKV_DIM + DN_DIM + DN_HEADS, QKV_DIM + DN_DIM + 2 * DN_HEADS], axis=-1)
    qkv_c, new_conv = causal_conv(qkv, conv_buf, conv_w)
    q, k, v = jnp.split(qkv_c.astype(jnp.float32), 3, axis=-1)
    q = l2norm(q.reshape(B, L, DN_HEADS, DN_HEAD_DIM)) * (DN_HEAD_DIM ** -0.5)
    k = l2norm(k.reshape(B, L, DN_HEADS, DN_HEAD_DIM))
    v = v.reshape(B, L, DN_HEADS, DN_HEAD_DIM)
    beta = jax.nn.sigmoid(b_raw.astype(jnp.float32))
    g = -jnp.exp(a_log.astype(jnp.float32)) * jax.nn.softplus(a_raw.astype(jnp.float32) + dt_bias.astype(jnp.float32))
    o, s_new = gated_delta_chunked(q, k, v, beta, g, s0.astype(jnp.float32), chunk)
    zf = z.astype(jnp.float32).reshape(B, L, DN_HEADS, DN_HEAD_DIM)
    o = o * lax.rsqrt(jnp.mean(o * o, axis=-1, keepdims=True) + RMS_EPS) * onorm_g.astype(jnp.float32) * jax.nn.silu(zf)
    o_dn = o.reshape(B, L, DN_DIM).astype(h.dtype)
    o_pool, new_pool = pool_mix(p, pool_buf, start_pos, pool_w, pool_scale)
    out = jnp.concatenate([o_dn, o_pool], axis=-1) @ w_out
    return out, s_new.astype(s0.dtype), new_conv.astype(conv_buf.dtype), new_pool.astype(pool_buf.dtype)


def layer(x, s0, conv_buf, pool_buf, start_pos, chunk, ln1_g, ln1_b, f1g, f1u, f1d, w_in, conv_w, a_log, dt_bias,
          onorm_g, pool_w, pool_scale, w_out, ln2_g, ln2_b, f2g, f2u, f2d, ln3_g, ln3_b):
    h = layer_norm(DN_ALPHA * x + 0.5 * swiglu(x, f1g, f1u, f1d), ln1_g, ln1_b)
    mix, s_new, c_new, p_new = mixer(h, s0, conv_buf, pool_buf, start_pos, chunk, w_in, conv_w, a_log, dt_bias,
                                     onorm_g, pool_w, pool_scale, w_out)
    h = layer_norm(DN_ALPHA * h + mix, ln2_g, ln2_b)
    y = layer_norm(DN_ALPHA * h + 0.5 * swiglu(h, f2g, f2u, f2d), ln3_g, ln3_b)
    return y, s_new, c_new, p_new


def setup_inputs(seed: int = 0) -> dict:
    key = jax.random.key(seed)
    ks = iter(jax.random.split(key, 40))
    f32 = jnp.float32
    nrm = lambda shape, scale: jax.random.normal(next(ks), shape, f32) * scale
    inp = {}
    inp['x_prompt'] = nrm((BATCH, SEQ, D_MODEL), 1.0)
    inp['x_sample'] = nrm((DEC_BATCH, DEC_SEQ, D_MODEL), 1.0)
    inp['state_delta'] = nrm((DEPTH, DEC_BATCH, DN_HEADS, DN_HEAD_DIM, DN_HEAD_DIM), 0.05)
    inp['state_conv'] = nrm((DEPTH, DEC_BATCH, CONV_W - 1, QKV_DIM), 1.0)
    inp['state_pool'] = nrm((DEPTH, DEC_BATCH, POOL_BUF, POOL_DIM), 1.0)
    inp['ln1_g'] = 1.0 + nrm((DEPTH, D_MODEL), 0.02)
    inp['ln1_b'] = nrm((DEPTH, D_MODEL), 0.02)
    inp['ffn1_w_gate'] = nrm((DEPTH, D_MODEL, D_FF), D_MODEL ** -0.5)
    inp['ffn1_w_up'] = nrm((DEPTH, D_MODEL, D_FF), D_MODEL ** -0.5)
    inp['ffn1_w_down'] = nrm((DEPTH, D_FF, D_MODEL), DN_BETA * D_FF ** -0.5)
    inp['w_in'] = nrm((DEPTH, D_MODEL, IN_DIM), D_MODEL ** -0.5)
    inp['conv_w'] = nrm((DEPTH, CONV_W, QKV_DIM), CONV_W ** -0.5)
    inp['a_log'] = jnp.log(jax.random.uniform(next(ks), (DEPTH, DN_HEADS), f32, 1.0, 16.0))
    dt = jnp.exp(jax.random.uniform(next(ks), (DEPTH, DN_HEADS), f32, math.log(1e-3), math.log(1e-1)))
    inp['dt_bias'] = dt + jnp.log(-jnp.expm1(-dt))
    inp['onorm_g'] = 1.0 + nrm((DEPTH, DN_HEAD_DIM), 0.02)
    inp['pool_w'] = nrm((DEPTH, POOL_GROUPS, POOL_GROUP_DIM, POOL_GROUP_DIM), POOL_GROUP_DIM ** -0.5)
    inp['pool_scale'] = 1.0 + nrm((DEPTH, POOL_DIM), 0.05)
    inp['w_out'] = nrm((DEPTH, MIX_DIM, D_MODEL), DN_BETA * MIX_DIM ** -0.5)
    inp['ln2_g'] = 1.0 + nrm((DEPTH, D_MODEL), 0.02)
    inp['ln2_b'] = nrm((DEPTH, D_MODEL), 0.02)
    inp['ffn2_w_gate'] = nrm((DEPTH, D_MODEL, D_FF), D_MODEL ** -0.5)
    inp['ffn2_w_up'] = nrm((DEPTH, D_MODEL, D_FF), D_MODEL ** -0.5)
    inp['ffn2_w_down'] = nrm((DEPTH, D_FF, D_MODEL), DN_BETA * D_FF ** -0.5)
    inp['ln3_g'] = 1.0 + nrm((DEPTH, D_MODEL), 0.02)
    inp['ln3_b'] = nrm((DEPTH, D_MODEL), 0.02)
    return inp


def reference(x_prompt, x_sample, state_delta, state_conv, state_pool, ln1_g, ln1_b, ffn1_w_gate, ffn1_w_up,
              ffn1_w_down, w_in, conv_w, a_log, dt_bias, onorm_g, pool_w, pool_scale, w_out, ln2_g, ln2_b,
              ffn2_w_gate, ffn2_w_up, ffn2_w_down, ln3_g, ln3_b):
    dt = x_prompt.dtype
    chunk_prompt = math.gcd(SEQ, CHUNK)
    chunk_sample = math.gcd(DEC_SEQ, CHUNK)
    xp, xs = x_prompt, x_sample
    dp, cp, pp, ds, cs_, ps = [], [], [], [], [], []
    for l in range(DEPTH):
        w = (ln1_g[l], ln1_b[l], ffn1_w_gate[l], ffn1_w_up[l], ffn1_w_down[l], w_in[l], conv_w[l], a_log[l],
             dt_bias[l], onorm_g[l], pool_w[l], pool_scale[l], w_out[l], ln2_g[l], ln2_b[l], ffn2_w_gate[l],
             ffn2_w_up[l], ffn2_w_down[l], ln3_g[l], ln3_b[l])
        s0 = jnp.zeros((BATCH, DN_HEADS, DN_HEAD_DIM, DN_HEAD_DIM), dt)
        c0 = jnp.zeros((BATCH, CONV_W - 1, QKV_DIM), dt)
        p0 = jnp.zeros((BATCH, POOL_BUF, POOL_DIM), dt)
        xp, s_p, c_p, p_p = layer(xp, s0, c0, p0, 0, chunk_prompt, *w)
        xs, s_s, c_s, p_s = layer(xs, state_delta[l], state_conv[l], state_pool[l], PAST_LEN, chunk_sample, *w)
        dp.append(s_p); cp.append(c_p); pp.append(p_p)
        ds.append(s_s); cs_.append(c_s); ps.append(p_s)
    delta_prompt = jnp.stack(dp)
    conv_prompt = jnp.stack(cp)
    pool_prompt = jnp.stack(pp)
    delta_sample = jnp.stack(ds)
    conv_sample = jnp.stack(cs_)
    pool_sample = jnp.stack(ps)
    return (xp, xs, delta_prompt, conv_prompt, pool_prompt, delta_sample, conv_sample, pool_sample)
```

```python
import functools

import jax
import jax.numpy as jnp
from jax import lax
from jax.experimental import pallas as pl
from jax.experimental.pallas import tpu as pltpu

F32 = jnp.float32
BF16 = jnp.bfloat16

D_MODEL = 1024
DEPTH = 4
DN_HEADS = 4
HEAD_DIM = 128
DN_DIM = DN_HEADS * HEAD_DIM
POOL_DIM = 512
QKV_DIM = 3 * DN_DIM
CONV_W = 4
POOL_WINDOWS = (2, 4, 8, 16)
POOL_GROUP_DIM = POOL_DIM // len(POOL_WINDOWS)
POOL_BUF = max(POOL_WINDOWS) - 1
D_FF = 2816
PAST_LEN = 16384
DN_ALPHA = (2.0 * DEPTH) ** 0.25
LN_EPS = 1e-5
RMS_EPS = 1e-6
L2_EPS = 1e-6

GATE_PAD = 128
PROJ_DIM = QKV_DIM + DN_DIM + POOL_DIM + GATE_PAD
Z_OFF = QKV_DIM
P_OFF = QKV_DIM + DN_DIM
G_OFF = QKV_DIM + DN_DIM + POOL_DIM
DECAY_LANE = DN_HEADS

CHUNK = 128
INV_BASE = 8
FFN_ROWS = 512
MIX_ROWS = 256
SAMPLE_TILE = 8


def _dot(a, b):
    return jnp.dot(a.astype(BF16), b.astype(BF16), preferred_element_type=F32)


def _dot_nt(a, b):
    return lax.dot_general(a.astype(BF16), b.astype(BF16), (((1,), (1,)), ((), ())),
                           preferred_element_type=F32)


def _dot_tn(a, b):
    return lax.dot_general(a.astype(BF16), b.astype(BF16), (((0,), (0,)), ((), ())),
                           preferred_element_type=F32)


def _split2(x):
    hi = x.astype(BF16)
    lo = (x - hi.astype(F32)).astype(BF16)
    return hi, lo


def _dot3(a, b):
    ah, al = _split2(a)
    bh, bl = _split2(b)
    d = functools.partial(jnp.dot, preferred_element_type=F32)
    return d(ah, bh) + (d(ah, bl) + d(al, bh))


def _dot3_nt(a, b):
    ah, al = _split2(a)
    bh, bl = _split2(b)
    d = functools.partial(lax.dot_general, dimension_numbers=(((1,), (1,)), ((), ())),
                          preferred_element_type=F32)
    return d(ah, bh) + (d(ah, bl) + d(al, bh))


def _dot_exact_lhs(a_bf16, b):
    b1 = b.astype(BF16)
    r1 = b - b1.astype(F32)
    b2 = r1.astype(BF16)
    b3 = (r1 - b2.astype(F32)).astype(BF16)
    d = functools.partial(jnp.dot, preferred_element_type=F32)
    return d(a_bf16, b1) + (d(a_bf16, b2) + d(a_bf16, b3))


def _silu(x):
    return x * jax.nn.sigmoid(x)


def _softplus(x):
    return jnp.maximum(x, 0.0) + jnp.log1p(jnp.exp(-jnp.abs(x)))


def _layer_norm(y, g, b):
    mu = jnp.mean(y, axis=-1, keepdims=True)
    yc = y - mu
    var = jnp.mean(yc * yc, axis=-1, keepdims=True)
    return yc * lax.rsqrt(var + LN_EPS) * g + b


def _l2norm(x):
    return x * lax.rsqrt(jnp.sum(x * x, axis=-1, keepdims=True) + L2_EPS)


def _ffn_ln_kernel(x_ref, wg_ref, wu_ref, wd_ref, g_ref, b_ref, o_ref):
    x = x_ref[...]
    xb = x.astype(BF16)
    gate = jnp.dot(xb, wg_ref[...], preferred_element_type=F32)
    up = jnp.dot(xb, wu_ref[...], preferred_element_type=F32)
    act = (_silu(gate) * up).astype(BF16)
    ff = jnp.dot(act, wd_ref[...], preferred_element_type=F32)
    o_ref[...] = _layer_norm(DN_ALPHA * x + 0.5 * ff, g_ref[...], b_ref[...])


def _resident(shape):
    zeros = (0,) * len(shape)
    return pl.BlockSpec(shape, lambda *_: zeros, pipeline_mode=pl.Buffered(1))


def _ffn_ln(x, wg, wu, wd, g, b):
    rows = x.shape[0]
    tm = min(FFN_ROWS, rows)
    assert rows % tm == 0
    return pl.pallas_call(
        _ffn_ln_kernel,
        out_shape=jax.ShapeDtypeStruct((rows, D_MODEL), F32),
        grid=(rows // tm,),
        in_specs=[
            pl.BlockSpec((tm, D_MODEL), lambda i: (i, 0)),
            _resident((D_MODEL, D_FF)),
            _resident((D_MODEL, D_FF)),
            _resident((D_FF, D_MODEL)),
            _resident((1, D_MODEL)),
            _resident((1, D_MODEL)),
        ],
        out_specs=pl.BlockSpec((tm, D_MODEL), lambda i: (i, 0)),
        compiler_params=pltpu.CompilerParams(dimension_semantics=("arbitrary",)),
        name="ffn_ln",
    )(x, wg, wu, wd, g, b)


def _gates(ba, alog, dtb):
    beta = jax.nn.sigmoid(ba)
    g = -jnp.exp(alog) * _softplus(ba + dtb)
    return beta, g


def _qkv_heads(conv_out):
    act = _silu(conv_out)
    qs, ks = [], []
    for h in range(DN_HEADS):
        lo = h * HEAD_DIM
        qs.append(_l2norm(act[:, lo:lo + HEAD_DIM]) * (HEAD_DIM ** -0.5))
        ks.append(_l2norm(act[:, DN_DIM + lo:DN_DIM + lo + HEAD_DIM]))
    return jnp.concatenate(qs, -1), jnp.concatenate(ks, -1), act[:, 2 * DN_DIM:]


def _gated_out_norm(o, z, onorm_g):
    outs = []
    for h in range(DN_HEADS):
        sl = slice(h * HEAD_DIM, (h + 1) * HEAD_DIM)
        oh = o[:, sl]
        oh = oh * lax.rsqrt(jnp.mean(oh * oh, axis=-1, keepdims=True) + RMS_EPS)
        outs.append(oh * onorm_g * _silu(z[:, sl]))
    return jnp.concatenate(outs, -1)


def _pool_project(d_groups, poolw_ref, pscale):
    ys = [_dot(d, poolw_ref[gi]) for gi, d in enumerate(d_groups)]
    return jnp.concatenate(ys, -1) * pscale


def _inv_unit_lower(m, row, col, size):
    eye = jnp.where(row == col, 1.0, 0.0).astype(F32)
    base_bits = INV_BASE.bit_length() - 1
    n0 = jnp.where((row >> base_bits) == (col >> base_bits), m, 0.0)
    x = eye - n0
    p = n0
    for _ in range(base_bits - 1):
        p = _dot3(p, p)
        x = x + _dot3(x, p)
    bits = base_bits
    while (1 << bits) < size:
        same_pair = (row >> (bits + 1)) == (col >> (bits + 1))
        lower_left = (((row >> bits) & 1) == 1) & (((col >> bits) & 1) == 0)
        a = jnp.where(same_pair & lower_left, m, 0.0)
        x = x - _dot3(x, _dot3(a, x))
        bits += 1
    return x


def _mixer_prompt_kernel(h_ref, win_ref, convw_ref, alog_ref, dtb_ref, onorm_ref, poolw_ref, pscale_ref,
                         wout_ref, g2_ref, b2_ref,
                         y_ref, sfin_ref, cfin_ref, pfin_ref,
                         s_scr, proj_scr, ext_scr, pext_scr, q_scr, k_scr, v_scr, beta_scr, g_scr, o_scr):
    tl = h_ref.shape[0]
    l = pl.program_id(1)
    n_l = pl.num_programs(1)

    @pl.when(l == 0)
    def _():
        s_scr[...] = jnp.zeros_like(s_scr)
        ext_scr[0:8, :] = jnp.zeros((8, QKV_DIM), F32)
        pext_scr[0:16, :] = jnp.zeros((16, POOL_DIM), F32)

    h = h_ref[...]
    proj_scr[...] = jnp.dot(h.astype(BF16), win_ref[...], preferred_element_type=F32)

    qkv = proj_scr[:, 0:QKV_DIM]
    ext_scr[8:8 + tl, :] = qkv
    conv = convw_ref[CONV_W - 1:CONV_W, :] * qkv
    for i in range(CONV_W - 1):
        conv = conv + convw_ref[i:i + 1, :] * ext_scr[5 + i:5 + i + tl, :]
    cfin_ref[...] = ext_scr[tl + 5:tl + 8, :]
    ext_scr[0:8, :] = ext_scr[tl:tl + 8, :]
    q, k, v = _qkv_heads(conv)
    q_scr[...] = q
    k_scr[...] = k
    v_scr[...] = v

    beta_t, g_t = _gates(proj_scr[:, G_OFF:G_OFF + GATE_PAD], alog_ref[...], dtb_ref[...])
    beta_scr[...] = beta_t
    g_scr[...] = g_t

    p = proj_scr[:, P_OFF:P_OFF + POOL_DIM]
    pext_scr[16:16 + tl, :] = p
    pos = l * tl + lax.broadcasted_iota(jnp.int32, (tl, 1), 0)
    d_groups = []
    for gi, w in enumerate(POOL_WINDOWS):
        sl = slice(gi * POOL_GROUP_DIM, (gi + 1) * POOL_GROUP_DIM)
        tok = p[:, sl]
        s = tok
        for i in range(1, w):
            s = s + pext_scr[16 - i:16 - i + tl, sl]
        cnt = jnp.minimum(w, pos + 1).astype(F32)
        d_groups.append(s / cnt - tok)
    o_pool = _pool_project(d_groups, poolw_ref, pscale_ref[...])
    pfin_ref[...] = pext_scr[tl + 1:tl + 16, :]
    pext_scr[0:16, :] = pext_scr[tl:tl + 16, :]

    c = CHUNK
    row = lax.broadcasted_iota(jnp.int32, (c, c), 0)
    col = lax.broadcasted_iota(jnp.int32, (c, c), 1)
    incl = row >= col
    strict = row > col
    tri = jnp.where(incl, 1.0, 0.0).astype(BF16)

    def chunk_body(ci, carry):
        r0 = pl.multiple_of(ci * c, c)
        rows = pl.ds(r0, c)
        gc_all = _dot_exact_lhs(tri, g_scr[rows, :])
        gc_t = gc_all.T
        beta_all = beta_scr[rows, :]
        for hd in range(DN_HEADS):
            sl = slice(hd * HEAD_DIM, (hd + 1) * HEAD_DIM)
            qh = q_scr[rows, sl]
            kh = k_scr[rows, sl]
            vh = v_scr[rows, sl]
            beta = beta_all[:, hd:hd + 1]
            gc_col = gc_all[:, DECAY_LANE + hd:DECAY_LANE + hd + 1]
            gc_row = gc_t[DECAY_LANE + hd:DECAY_LANE + hd + 1, :]
            gc_last = gc_all[c - 1:c, DECAY_LANE + hd:DECAY_LANE + hd + 1]
            diff = gc_col - gc_row
            dec = jnp.where(incl, jnp.exp(jnp.where(incl, diff, 0.0)), 0.0)
            kk = _dot3_nt(kh, kh)
            qk = _dot_nt(qh, kh) * dec
            m = jnp.where(strict, beta * kk * dec, 0.0)
            t_inv = _inv_unit_lower(m, row, col, c)
            eg = jnp.exp(gc_col)
            rhs = jnp.concatenate([beta * vh, (beta * eg) * kh], axis=-1)
            sol = _dot3(t_inv, rhs)
            uv = sol[:, :HEAD_DIM]
            wk = sol[:, HEAD_DIM:]
            q_dec = qh * eg
            k_end = kh * jnp.exp(gc_last - gc_col)
            g_end = jnp.exp(gc_last)
            s_old = s_scr[hd]
            ws = _dot(jnp.concatenate([wk, q_dec], axis=0), s_old)
            u = uv - ws[:c]
            o_scr[rows, sl] = ws[c:] + _dot(qk, u)
            s_scr[hd] = g_end * s_old + _dot_tn(k_end, u)
        return carry

    lax.fori_loop(0, tl // c, chunk_body, 0)

    @pl.when(l == n_l - 1)
    def _():
        sfin_ref[...] = s_scr[...]

    o_dn = _gated_out_norm(o_scr[...], proj_scr[:, Z_OFF:Z_OFF + DN_DIM], onorm_ref[...])
    mix = _dot(jnp.concatenate([o_dn, o_pool], axis=-1), wout_ref[...])
    y_ref[...] = _layer_norm(DN_ALPHA * h + mix, g2_ref[...], b2_ref[...])


def _mixer_prompt(h, win, convw, alog, dtb, onorm, poolw, pscale, wout, g2, b2):
    bsz, seq, _ = h.shape
    tl = MIX_ROWS
    assert seq % tl == 0 and tl % CHUNK == 0
    out_shape = (
        jax.ShapeDtypeStruct((bsz, seq, D_MODEL), F32),
        jax.ShapeDtypeStruct((bsz, DN_HEADS, HEAD_DIM, HEAD_DIM), F32),
        jax.ShapeDtypeStruct((bsz, CONV_W - 1, QKV_DIM), F32),
        jax.ShapeDtypeStruct((bsz, POOL_BUF, POOL_DIM), F32),
    )
    return pl.pallas_call(
        _mixer_prompt_kernel,
        out_shape=out_shape,
        grid=(bsz, seq // tl),
        in_specs=[
            pl.BlockSpec((None, tl, D_MODEL), lambda b, l: (b, l, 0)),
            _resident((D_MODEL, PROJ_DIM)),
            _resident((CONV_W, QKV_DIM)),
            _resident((1, GATE_PAD)),
            _resident((1, GATE_PAD)),
            _resident((1, HEAD_DIM)),
            _resident((len(POOL_WINDOWS), POOL_GROUP_DIM, POOL_GROUP_DIM)),
            _resident((1, POOL_DIM)),
            _resident((DN_DIM + POOL_DIM, D_MODEL)),
            _resident((1, D_MODEL)),
            _resident((1, D_MODEL)),
        ],
        out_specs=(
            pl.BlockSpec((None, tl, D_MODEL), lambda b, l: (b, l, 0)),
            pl.BlockSpec((None, DN_HEADS, HEAD_DIM, HEAD_DIM), lambda b, l: (b, 0, 0, 0)),
            pl.BlockSpec((None, CONV_W - 1, QKV_DIM), lambda b, l: (b, 0, 0)),
            pl.BlockSpec((None, POOL_BUF, POOL_DIM), lambda b, l: (b, 0, 0)),
        ),
        scratch_shapes=[
            pltpu.VMEM((DN_HEADS, HEAD_DIM, HEAD_DIM), F32),
            pltpu.VMEM((tl, PROJ_DIM), F32),
            pltpu.VMEM((tl + 8, QKV_DIM), F32),
            pltpu.VMEM((tl + 16, POOL_DIM), F32),
            pltpu.VMEM((tl, DN_DIM), F32),
            pltpu.VMEM((tl, DN_DIM), F32),
            pltpu.VMEM((tl, DN_DIM), F32),
            pltpu.VMEM((tl, GATE_PAD), F32),
            pltpu.VMEM((tl, GATE_PAD), F32),
            pltpu.VMEM((tl, DN_DIM), F32),
        ],
        compiler_params=pltpu.CompilerParams(dimension_semantics=("arbitrary", "arbitrary")),
        name="mixer_prompt",
    )(h, win, convw, alog, dtb, onorm, poolw, pscale, wout, g2, b2)


def _mixer_sample_kernel(h_ref, s_ref, cbuf_ref, pbuf_ref, win_ref, convw_ref, alog_ref, dtb_ref, onorm_ref,
                         poolw_ref, pscale_ref, wout_ref, g2_ref, b2_ref,
                         y_ref, snew_ref, cnew_ref, pnew_ref,
                         q_scr, k_scr, v_scr, beta_scr, eg_scr, z_scr, opool_scr, o_scr):
    i = pl.program_id(0)
    n_i = pl.num_programs(0)

    @pl.when(i == 0)
    def _():
        proj = jnp.dot(h_ref[...].astype(BF16), win_ref[...], preferred_element_type=F32)
        qkv = proj[:, 0:QKV_DIM]
        cb = cbuf_ref[...]
        conv = convw_ref[CONV_W - 1:CONV_W, :] * qkv
        for j in range(CONV_W - 1):
            conv = conv + convw_ref[j:j + 1, :] * cb[:, j * QKV_DIM:(j + 1) * QKV_DIM]
        cnew_ref[...] = jnp.concatenate([cb[:, QKV_DIM:], qkv], axis=-1)
        q, k, v = _qkv_heads(conv)
        q_scr[...] = q
        k_scr[...] = k
        v_scr[...] = v
        beta_t, g_t = _gates(proj[:, G_OFF:G_OFF + GATE_PAD], alog_ref[...], dtb_ref[...])
        beta_scr[...] = beta_t
        eg_scr[...] = jnp.exp(g_t)
        z_scr[...] = proj[:, Z_OFF:Z_OFF + DN_DIM]

        p = proj[:, P_OFF:P_OFF + POOL_DIM]
        pb = pbuf_ref[...]
        d_groups = []
        for gi, w in enumerate(POOL_WINDOWS):
            lo = gi * POOL_GROUP_DIM
            tok = p[:, lo:lo + POOL_GROUP_DIM]
            s = tok
            for back in range(1, w):
                off = (POOL_BUF - back) * POOL_DIM + lo
                s = s + pb[:, off:off + POOL_GROUP_DIM]
            cnt = float(min(w, PAST_LEN + 1))
            d_groups.append(s / cnt - tok)
        opool_scr[...] = _pool_project(d_groups, poolw_ref, pscale_ref[...])
        pnew_ref[...] = jnp.concatenate([pb[:, POOL_DIM:], p], axis=-1)

    eye = jnp.where(lax.broadcasted_iota(jnp.int32, (HEAD_DIM, HEAD_DIM), 0)
                    == lax.broadcasted_iota(jnp.int32, (HEAD_DIM, HEAD_DIM), 1), 1.0, 0.0).astype(F32)

    rows = pl.ds(pl.multiple_of(i * SAMPLE_TILE, SAMPLE_TILE), SAMPLE_TILE)
    q_t, k_t, v_t = q_scr[rows, :], k_scr[rows, :], v_scr[rows, :]
    beta_t, a_t = beta_scr[rows, :], eg_scr[rows, :]
    o_rows = []
    for j in range(SAMPLE_TILE):
        o_heads = []
        for hd in range(DN_HEADS):
            sl = slice(hd * HEAD_DIM, (hd + 1) * HEAD_DIM)
            qh = q_t[j:j + 1, sl]
            kh = k_t[j:j + 1, sl]
            vh = v_t[j:j + 1, sl]
            beta = beta_t[j:j + 1, hd:hd + 1]
            a = a_t[j:j + 1, DECAY_LANE + hd:DECAY_LANE + hd + 1]
            k_col = jnp.sum(eye * kh, axis=-1, keepdims=True)
            q_col = jnp.sum(eye * qh, axis=-1, keepdims=True)
            s_old = s_ref[j, hd]
            ks = jnp.sum(k_col * s_old, axis=0, keepdims=True)
            qs = jnp.sum(q_col * s_old, axis=0, keepdims=True)
            u = beta * vh - (beta * a) * ks
            qk = jnp.sum(qh * kh, axis=-1, keepdims=True)
            o_heads.append(a * qs + qk * u)
            snew_ref[j, hd] = a * s_old + k_col * u
        o_rows.append(jnp.concatenate(o_heads, axis=-1))
    o_scr[rows, :] = jnp.concatenate(o_rows, axis=0)

    @pl.when(i == n_i - 1)
    def _():
        o_dn = _gated_out_norm(o_scr[...], z_scr[...], onorm_ref[...])
        mix = _dot(jnp.concatenate([o_dn, opool_scr[...]], axis=-1), wout_ref[...])
        y_ref[...] = _layer_norm(DN_ALPHA * h_ref[...] + mix, g2_ref[...], b2_ref[...])


def _mixer_sample(h, s0, cbuf, pbuf, win, convw, alog, dtb, onorm, poolw, pscale, wout, g2, b2):
    bsz = h.shape[0]
    assert bsz % SAMPLE_TILE == 0
    cb2 = cbuf.reshape(bsz, (CONV_W - 1) * QKV_DIM)
    pb2 = pbuf.reshape(bsz, POOL_BUF * POOL_DIM)
    out_shape = (
        jax.ShapeDtypeStruct((bsz, D_MODEL), F32),
        jax.ShapeDtypeStruct(s0.shape, F32),
        jax.ShapeDtypeStruct(cb2.shape, F32),
        jax.ShapeDtypeStruct(pb2.shape, F32),
    )
    state_spec = pl.BlockSpec((SAMPLE_TILE, DN_HEADS, HEAD_DIM, HEAD_DIM), lambda i: (i, 0, 0, 0))
    y, s_new, c_new, p_new = pl.pallas_call(
        _mixer_sample_kernel,
        out_shape=out_shape,
        grid=(bsz // SAMPLE_TILE,),
        in_specs=[
            _resident((bsz, D_MODEL)),
            state_spec,
            _resident(cb2.shape),
            _resident(pb2.shape),
            _resident((D_MODEL, PROJ_DIM)),
            _resident((CONV_W, QKV_DIM)),
            _resident((1, GATE_PAD)),
            _resident((1, GATE_PAD)),
            _resident((1, HEAD_DIM)),
            _resident((len(POOL_WINDOWS), POOL_GROUP_DIM, POOL_GROUP_DIM)),
            _resident((1, POOL_DIM)),
            _resident((DN_DIM + POOL_DIM, D_MODEL)),
            _resident((1, D_MODEL)),
            _resident((1, D_MODEL)),
        ],
        out_specs=(
            pl.BlockSpec((bsz, D_MODEL), lambda i: (0, 0)),
            state_spec,
            pl.BlockSpec(cb2.shape, lambda i: (0, 0)),
            pl.BlockSpec(pb2.shape, lambda i: (0, 0)),
        ),
        scratch_shapes=[
            pltpu.VMEM((bsz, DN_DIM), F32),
            pltpu.VMEM((bsz, DN_DIM), F32),
            pltpu.VMEM((bsz, DN_DIM), F32),
            pltpu.VMEM((bsz, GATE_PAD), F32),
            pltpu.VMEM((bsz, GATE_PAD), F32),
            pltpu.VMEM((bsz, DN_DIM), F32),
            pltpu.VMEM((bsz, POOL_DIM), F32),
            pltpu.VMEM((bsz, DN_DIM), F32),
        ],
        compiler_params=pltpu.CompilerParams(dimension_semantics=("arbitrary",)),
        name="mixer_sample",
    )(h, s0, cb2, pb2, win, convw, alog, dtb, onorm, poolw, pscale, wout, g2, b2)
    return y, s_new, c_new.reshape(cbuf.shape), p_new.reshape(pbuf.shape)


def _pad_gate_param(v):
    out = jnp.zeros((v.shape[0], 1, GATE_PAD), F32)
    return out.at[:, 0, DECAY_LANE:DECAY_LANE + DN_HEADS].set(v.astype(F32))


def kernel(x_prompt, x_sample, state_delta, state_conv, state_pool, ln1_g, ln1_b, ffn1_w_gate, ffn1_w_up, ffn1_w_down, w_in, conv_w, a_log, dt_bias, onorm_g, pool_w, pool_scale, w_out, ln2_g, ln2_b, ffn2_w_gate, ffn2_w_up, ffn2_w_down, ln3_g, ln3_b):
    bsz, seq, _ = x_prompt.shape
    dec_b = x_sample.shape[0]

    b_off = QKV_DIM + DN_DIM
    p_off = b_off + 2 * DN_HEADS
    win = jnp.concatenate(
        [w_in[:, :, :b_off], w_in[:, :, p_off:], w_in[:, :, b_off:p_off],
         jnp.zeros((DEPTH, D_MODEL, GATE_PAD - 2 * DN_HEADS), w_in.dtype)], axis=-1).astype(BF16)
    wout = w_out.astype(BF16)
    poolw = pool_w.astype(BF16)
    f1g, f1u, f1d = ffn1_w_gate.astype(BF16), ffn1_w_up.astype(BF16), ffn1_w_down.astype(BF16)
    f2g, f2u, f2d = ffn2_w_gate.astype(BF16), ffn2_w_up.astype(BF16), ffn2_w_down.astype(BF16)
    alog = _pad_gate_param(a_log)
    dtb = _pad_gate_param(dt_bias)
    row = lambda v: v.reshape(DEPTH, 1, -1)
    ln1g, ln1b, ln2g, ln2b, ln3g, ln3b = map(row, (ln1_g, ln1_b, ln2_g, ln2_b, ln3_g, ln3_b))
    onorm = row(onorm_g)
    pscale = row(pool_scale)

    xp = x_prompt.reshape(bsz * seq, D_MODEL)
    xs = x_sample.reshape(dec_b, D_MODEL)
    dp, cp, pp, ds, cs, ps = [], [], [], [], [], []
    for l in range(DEPTH):
        mix_w = (win[l], conv_w[l], alog[l], dtb[l], onorm[l], poolw[l], pscale[l], wout[l], ln2g[l], ln2b[l])
        hp = _ffn_ln(xp, f1g[l], f1u[l], f1d[l], ln1g[l], ln1b[l])
        hs = _ffn_ln(xs, f1g[l], f1u[l], f1d[l], ln1g[l], ln1b[l])
        hp, s_p, c_p, p_p = _mixer_prompt(hp.reshape(bsz, seq, D_MODEL), *mix_w)
        hs, s_s, c_s, p_s = _mixer_sample(hs, state_delta[l], state_conv[l], state_pool[l], *mix_w)
        xp = _ffn_ln(hp.reshape(bsz * seq, D_MODEL), f2g[l], f2u[l], f2d[l], ln3g[l], ln3b[l])
        xs = _ffn_ln(hs, f2g[l], f2u[l], f2d[l], ln3g[l], ln3b[l])
        dp.append(s_p); cp.append(c_p); pp.append(p_p)
        ds.append(s_s); cs.append(c_s); ps.append(p_s)
    return (xp.reshape(bsz, seq, D_MODEL), xs.reshape(dec_b, 1, D_MODEL),
            jnp.stack(dp), jnp.stack(cp), jnp.stack(pp), jnp.stack(ds), jnp.stack(cs), jnp.stack(ps))
```

```python
import functools

import jax
import jax.numpy as jnp
from jax import lax
from jax.experimental import pallas as pl
from jax.experimental.pallas import tpu as pltpu

F32 = jnp.float32
BF16 = jnp.bfloat16

D_MODEL = 1024
DEPTH = 4
DN_HEADS = 4
HEAD_DIM = 128
DN_DIM = DN_HEADS * HEAD_DIM
POOL_DIM = 512
QKV_DIM = 3 * DN_DIM
CONV_W = 4
POOL_WINDOWS = (2, 4, 8, 16)
POOL_GROUP_DIM = POOL_DIM // len(POOL_WINDOWS)
POOL_BUF = max(POOL_WINDOWS) - 1
D_FF = 2816
PAST_LEN = 16384
DN_ALPHA = (2.0 * DEPTH) ** 0.25
LN_EPS = 1e-5
RMS_EPS = 1e-6
L2_EPS = 1e-6

GATE_PAD = 128
PROJ_DIM = QKV_DIM + DN_DIM + POOL_DIM + GATE_PAD
Z_OFF = QKV_DIM
P_OFF = QKV_DIM + DN_DIM
G_OFF = QKV_DIM + DN_DIM + POOL_DIM
DECAY_LANE = DN_HEADS

CHUNK = 128
INV_BASE = 8
FFN_ROWS = 512
MIX_ROWS = 256
SAMPLE_TILE = 8


def _dot(a, b):
    return jnp.dot(a.astype(BF16), b.astype(BF16), preferred_element_type=F32)


def _dot_nt(a, b):
    return lax.dot_general(a.astype(BF16), b.astype(BF16), (((1,), (1,)), ((), ())),
                           preferred_element_type=F32)


def _dot_tn(a, b):
    return lax.dot_general(a.astype(BF16), b.astype(BF16), (((0,), (0,)), ((), ())),
                           preferred_element_type=F32)


def _dot_exact_lhs(a_bf16, b):
    b1 = b.astype(BF16)
    r1 = b - b1.astype(F32)
    b2 = r1.astype(BF16)
    b3 = (r1 - b2.astype(F32)).astype(BF16)
    d = functools.partial(jnp.dot, preferred_element_type=F32)
    return d(a_bf16, b1) + (d(a_bf16, b2) + d(a_bf16, b3))


def _silu(x):
    return x * jax.nn.sigmoid(x)


def _softplus(x):
    return jnp.maximum(x, 0.0) + jnp.log1p(jnp.exp(-jnp.abs(x)))


def _layer_norm(y, g, b):
    mu = jnp.mean(y, axis=-1, keepdims=True)
    yc = y - mu
    var = jnp.mean(yc * yc, axis=-1, keepdims=True)
    return yc * lax.rsqrt(var + LN_EPS) * g + b


def _l2norm(x):
    return x * lax.rsqrt(jnp.sum(x * x, axis=-1, keepdims=True) + L2_EPS)


def _ffn_ln_kernel(x_ref, wg_ref, wu_ref, wd_ref, g_ref, b_ref, o_ref):
    x = x_ref[...]
    xb = x.astype(BF16)
    gate = jnp.dot(xb, wg_ref[...], preferred_element_type=F32)
    up = jnp.dot(xb, wu_ref[...], preferred_element_type=F32)
    act = (_silu(gate) * up).astype(BF16)
    ff = jnp.dot(act, wd_ref[...], preferred_element_type=F32)
    o_ref[...] = _layer_norm(DN_ALPHA * x + 0.5 * ff, g_ref[...], b_ref[...])


def _resident(shape):
    zeros = (0,) * len(shape)
    return pl.BlockSpec(shape, lambda *_: zeros, pipeline_mode=pl.Buffered(1))


def _ffn_ln(x, wg, wu, wd, g, b):
    rows = x.shape[0]
    tm = min(FFN_ROWS, rows)
    assert rows % tm == 0
    return pl.pallas_call(
        _ffn_ln_kernel,
        out_shape=jax.ShapeDtypeStruct((rows, D_MODEL), F32),
        grid=(rows // tm,),
        in_specs=[
            pl.BlockSpec((tm, D_MODEL), lambda i: (i, 0)),
            _resident((D_MODEL, D_FF)),
            _resident((D_MODEL, D_FF)),
            _resident((D_FF, D_MODEL)),
            _resident((1, D_MODEL)),
            _resident((1, D_MODEL)),
        ],
        out_specs=pl.BlockSpec((tm, D_MODEL), lambda i: (i, 0)),
        compiler_params=pltpu.CompilerParams(dimension_semantics=("arbitrary",)),
        name="ffn_ln",
    )(x, wg, wu, wd, g, b)


def _gates(ba, alog, dtb):
    beta = jax.nn.sigmoid(ba)
    g = -jnp.exp(alog) * _softplus(ba + dtb)
    return beta, g


def _qkv_heads(conv_out):
    act = _silu(conv_out)
    qs, ks = [], []
    for h in range(DN_HEADS):
        lo = h * HEAD_DIM
        qs.append(_l2norm(act[:, lo:lo + HEAD_DIM]) * (HEAD_DIM ** -0.5))
        ks.append(_l2norm(act[:, DN_DIM + lo:DN_DIM + lo + HEAD_DIM]))
    return jnp.concatenate(qs, -1), jnp.concatenate(ks, -1), act[:, 2 * DN_DIM:]


def _gated_out_norm(o, z, onorm_g):
    outs = []
    for h in range(DN_HEADS):
        sl = slice(h * HEAD_DIM, (h + 1) * HEAD_DIM)
        oh = o[:, sl]
        oh = oh * lax.rsqrt(jnp.mean(oh * oh, axis=-1, keepdims=True) + RMS_EPS)
        outs.append(oh * onorm_g * _silu(z[:, sl]))
    return jnp.concatenate(outs, -1)


def _pool_project(d_groups, poolw_ref, pscale):
    ys = [_dot(d, poolw_ref[gi]) for gi, d in enumerate(d_groups)]
    return jnp.concatenate(ys, -1) * pscale


def _inv_unit_lower(ms, row, col, size):
    eye = jnp.where(row == col, 1.0, 0.0).astype(F32)
    base_bits = INV_BASE.bit_length() - 1
    base_mask = (row >> base_bits) == (col >> base_bits)
    ps = [jnp.where(base_mask, m, 0.0) for m in ms]
    xs = [eye - p for p in ps]
    for _ in range(base_bits - 1):
        ps = [_dot(p, p) for p in ps]
        xs = [x + _dot(x, p) for x, p in zip(xs, ps)]
    bits = base_bits
    while (1 << bits) < size:
        same_pair = (row >> (bits + 1)) == (col >> (bits + 1))
        lower_left = (((row >> bits) & 1) == 1) & (((col >> bits) & 1) == 0)
        mask = same_pair & lower_left
        ys = [_dot(jnp.where(mask, m, 0.0), x) for m, x in zip(ms, xs)]
        xs = [x - _dot(x, y) for x, y in zip(xs, ys)]
        bits += 1
    return xs


def _mixer_prompt_kernel(h_ref, win_ref, convw_ref, alog_ref, dtb_ref, onorm_ref, poolw_ref, pscale_ref,
                         wout_ref, g2_ref, b2_ref,
                         y_ref, sfin_ref, cfin_ref, pfin_ref,
                         s_scr, proj_scr, ext_scr, pext_scr, q_scr, k_scr, v_scr, beta_scr, g_scr, o_scr):
    tl = h_ref.shape[0]
    l = pl.program_id(1)
    n_l = pl.num_programs(1)

    @pl.when(l == 0)
    def _():
        s_scr[...] = jnp.zeros_like(s_scr)
        ext_scr[0:8, :] = jnp.zeros((8, QKV_DIM), F32)
        pext_scr[0:16, :] = jnp.zeros((16, POOL_DIM), F32)

    h = h_ref[...]
    proj_scr[...] = jnp.dot(h.astype(BF16), win_ref[...], preferred_element_type=F32)

    qkv = proj_scr[:, 0:QKV_DIM]
    ext_scr[8:8 + tl, :] = qkv
    conv = convw_ref[CONV_W - 1:CONV_W, :] * qkv
    for i in range(CONV_W - 1):
        conv = conv + convw_ref[i:i + 1, :] * ext_scr[5 + i:5 + i + tl, :]
    cfin_ref[...] = ext_scr[tl + 5:tl + 8, :]
    ext_scr[0:8, :] = ext_scr[tl:tl + 8, :]
    q, k, v = _qkv_heads(conv)
    q_scr[...] = q
    k_scr[...] = k
    v_scr[...] = v

    beta_t, g_t = _gates(proj_scr[:, G_OFF:G_OFF + GATE_PAD], alog_ref[...], dtb_ref[...])
    beta_scr[...] = beta_t
    g_scr[...] = g_t

    p = proj_scr[:, P_OFF:P_OFF + POOL_DIM]
    pext_scr[16:16 + tl, :] = p
    pos = l * tl + lax.broadcasted_iota(jnp.int32, (tl, 1), 0)
    d_groups = []
    for gi, w in enumerate(POOL_WINDOWS):
        sl = slice(gi * POOL_GROUP_DIM, (gi + 1) * POOL_GROUP_DIM)
        tok = p[:, sl]
        s = tok
        for i in range(1, w):
            s = s + pext_scr[16 - i:16 - i + tl, sl]
        cnt = jnp.minimum(w, pos + 1).astype(F32)
        d_groups.append(s / cnt - tok)
    o_pool = _pool_project(d_groups, poolw_ref, pscale_ref[...])
    pfin_ref[...] = pext_scr[tl + 1:tl + 16, :]
    pext_scr[0:16, :] = pext_scr[tl:tl + 16, :]

    c = CHUNK
    row = lax.broadcasted_iota(jnp.int32, (c, c), 0)
    col = lax.broadcasted_iota(jnp.int32, (c, c), 1)
    incl = row >= col
    strict = row > col
    tri = jnp.where(incl, 1.0, 0.0).astype(BF16)

    n_chunks = tl // c
    chains = [(ci, hd) for ci in range(n_chunks) for hd in range(DN_HEADS)]
    gc_alls, gc_ts, beta_alls = [], [], []
    for ci in range(n_chunks):
        rows = slice(ci * c, (ci + 1) * c)
        gc_all = _dot_exact_lhs(tri, g_scr[rows, :])
        gc_alls.append(gc_all)
        gc_ts.append(gc_all.T)
        beta_alls.append(beta_scr[rows, :])
    qs, ks, betas, egs, decs, rhss, k_ends, g_ends = [], [], [], [], [], [], [], []
    for ci, hd in chains:
        rows = slice(ci * c, (ci + 1) * c)
        sl = slice(hd * HEAD_DIM, (hd + 1) * HEAD_DIM)
        lane = DECAY_LANE + hd
        qh, kh, vh = q_scr[rows, sl], k_scr[rows, sl], v_scr[rows, sl]
        beta = beta_alls[ci][:, hd:hd + 1]
        gc_col = gc_alls[ci][:, lane:lane + 1]
        gc_row = gc_ts[ci][lane:lane + 1, :]
        gc_last = gc_alls[ci][c - 1:c, lane:lane + 1]
        eg = jnp.exp(gc_col)
        decs.append(jnp.where(incl, jnp.exp(jnp.where(incl, gc_col - gc_row, 0.0)), 0.0))
        rhss.append(jnp.concatenate([beta * vh, (beta * eg) * kh], axis=-1))
        k_ends.append(kh * jnp.exp(gc_last - gc_col))
        g_ends.append(jnp.exp(gc_last))
        qs.append(qh); ks.append(kh); betas.append(beta); egs.append(eg)
    kqs = [_dot_nt(jnp.concatenate([kh, qh], axis=0), kh) for kh, qh in zip(ks, qs)]
    ms = [jnp.where(strict, beta * kq[:c] * dec, 0.0) for beta, kq, dec in zip(betas, kqs, decs)]
    qks = [kq[c:] * dec for kq, dec in zip(kqs, decs)]
    t_invs = _inv_unit_lower(ms, row, col, c)
    sols = [_dot(t_inv, rhs) for t_inv, rhs in zip(t_invs, rhss)]
    q_decs = [qh * eg for qh, eg in zip(qs, egs)]

    s_cur = [s_scr[hd] for hd in range(DN_HEADS)]
    for ci in range(n_chunks):
        idx = [ci * DN_HEADS + hd for hd in range(DN_HEADS)]
        wss = [_dot(jnp.concatenate([sols[i][:, HEAD_DIM:], q_decs[i]], axis=0), s) for i, s in zip(idx, s_cur)]
        us = [sols[i][:, :HEAD_DIM] - ws[:c] for i, ws in zip(idx, wss)]
        outs = [ws[c:] + _dot(qks[i], u) for i, ws, u in zip(idx, wss, us)]
        s_cur = [g_ends[i] * s + _dot_tn(k_ends[i], u) for i, s, u in zip(idx, s_cur, us)]
        o_scr[ci * c:(ci + 1) * c, :] = jnp.concatenate(outs, axis=-1)
    for hd in range(DN_HEADS):
        s_scr[hd] = s_cur[hd]

    @pl.when(l == n_l - 1)
    def _():
        sfin_ref[...] = s_scr[...]

    o_dn = _gated_out_norm(o_scr[...], proj_scr[:, Z_OFF:Z_OFF + DN_DIM], onorm_ref[...])
    mix = _dot(jnp.concatenate([o_dn, o_pool], axis=-1), wout_ref[...])
    y_ref[...] = _layer_norm(DN_ALPHA * h + mix, g2_ref[...], b2_ref[...])


def _mixer_prompt(h, win, convw, alog, dtb, onorm, poolw, pscale, wout, g2, b2):
    bsz, seq, _ = h.shape
    tl = MIX_ROWS
    assert seq % tl == 0 and tl % CHUNK == 0
    out_shape = (
        jax.ShapeDtypeStruct((bsz, seq, D_MODEL), F32),
        jax.ShapeDtypeStruct((bsz, DN_HEADS, HEAD_DIM, HEAD_DIM), F32),
        jax.ShapeDtypeStruct((bsz, CONV_W - 1, QKV_DIM), F32),
        jax.ShapeDtypeStruct((bsz, POOL_BUF, POOL_DIM), F32),
    )
    return pl.pallas_call(
        _mixer_prompt_kernel,
        out_shape=out_shape,
        grid=(bsz, seq // tl),
        in_specs=[
            pl.BlockSpec((None, tl, D_MODEL), lambda b, l: (b, l, 0)),
            _resident((D_MODEL, PROJ_DIM)),
            _resident((CONV_W, QKV_DIM)),
            _resident((1, GATE_PAD)),
            _resident((1, GATE_PAD)),
            _resident((1, HEAD_DIM)),
            _resident((len(POOL_WINDOWS), POOL_GROUP_DIM, POOL_GROUP_DIM)),
            _resident((1, POOL_DIM)),
            _resident((DN_DIM + POOL_DIM, D_MODEL)),
            _resident((1, D_MODEL)),
            _resident((1, D_MODEL)),
        ],
        out_specs=(
            pl.BlockSpec((None, tl, D_MODEL), lambda b, l: (b, l, 0)),
            pl.BlockSpec((None, DN_HEADS, HEAD_DIM, HEAD_DIM), lambda b, l: (b, 0, 0, 0)),
            pl.BlockSpec((None, CONV_W - 1, QKV_DIM), lambda b, l: (b, 0, 0)),
            pl.BlockSpec((None, POOL_BUF, POOL_DIM), lambda b, l: (b, 0, 0)),
        ),
        scratch_shapes=[
            pltpu.VMEM((DN_HEADS, HEAD_DIM, HEAD_DIM), F32),
            pltpu.VMEM((tl, PROJ_DIM), F32),
            pltpu.VMEM((tl + 8, QKV_DIM), F32),
            pltpu.VMEM((tl + 16, POOL_DIM), F32),
            pltpu.VMEM((tl, DN_DIM), F32),
            pltpu.VMEM((tl, DN_DIM), F32),
            pltpu.VMEM((tl, DN_DIM), F32),
            pltpu.VMEM((tl, GATE_PAD), F32),
            pltpu.VMEM((tl, GATE_PAD), F32),
            pltpu.VMEM((tl, DN_DIM), F32),
        ],
        compiler_params=pltpu.CompilerParams(dimension_semantics=("arbitrary", "arbitrary")),
        name="mixer_prompt",
    )(h, win, convw, alog, dtb, onorm, poolw, pscale, wout, g2, b2)


def _mixer_sample_kernel(h_ref, s_ref, cbuf_ref, pbuf_ref, win_ref, convw_ref, alog_ref, dtb_ref, onorm_ref,
                         poolw_ref, pscale_ref, wout_ref, g2_ref, b2_ref,
                         y_ref, snew_ref, cnew_ref, pnew_ref,
                         q_scr, k_scr, v_scr, beta_scr, eg_scr, z_scr, opool_scr, o_scr):
    i = pl.program_id(0)
    n_i = pl.num_programs(0)

    @pl.when(i == 0)
    def _():
        proj = jnp.dot(h_ref[...].astype(BF16), win_ref[...], preferred_element_type=F32)
        qkv = proj[:, 0:QKV_DIM]
        cb = cbuf_ref[...]
        conv = convw_ref[CONV_W - 1:CONV_W, :] * qkv
        for j in range(CONV_W - 1):
            conv = conv + convw_ref[j:j + 1, :] * cb[:, j * QKV_DIM:(j + 1) * QKV_DIM]
        cnew_ref[...] = jnp.concatenate([cb[:, QKV_DIM:], qkv], axis=-1)
        q, k, v = _qkv_heads(conv)
        q_scr[...] = q
        k_scr[...] = k
        v_scr[...] = v
        beta_t, g_t = _gates(proj[:, G_OFF:G_OFF + GATE_PAD], alog_ref[...], dtb_ref[...])
        beta_scr[...] = beta_t
        eg_scr[...] = jnp.exp(g_t)
        z_scr[...] = proj[:, Z_OFF:Z_OFF + DN_DIM]

        p = proj[:, P_OFF:P_OFF + POOL_DIM]
        pb = pbuf_ref[...]
        d_groups = []
        for gi, w in enumerate(POOL_WINDOWS):
            lo = gi * POOL_GROUP_DIM
            tok = p[:, lo:lo + POOL_GROUP_DIM]
            s = tok
            for back in range(1, w):
                off = (POOL_BUF - back) * POOL_DIM + lo
                s = s + pb[:, off:off + POOL_GROUP_DIM]
            cnt = float(min(w, PAST_LEN + 1))
            d_groups.append(s / cnt - tok)
        opool_scr[...] = _pool_project(d_groups, poolw_ref, pscale_ref[...])
        pnew_ref[...] = jnp.concatenate([pb[:, POOL_DIM:], p], axis=-1)

    eye = jnp.where(lax.broadcasted_iota(jnp.int32, (HEAD_DIM, HEAD_DIM), 0)
                    == lax.broadcasted_iota(jnp.int32, (HEAD_DIM, HEAD_DIM), 1), 1.0, 0.0).astype(F32)

    rows = pl.ds(pl.multiple_of(i * SAMPLE_TILE, SAMPLE_TILE), SAMPLE_TILE)
    q_t, k_t, v_t = q_scr[rows, :], k_scr[rows, :], v_scr[rows, :]
    beta_t, a_t = beta_scr[rows, :], eg_scr[rows, :]
    o_rows = []
    for j in range(SAMPLE_TILE):
        o_heads = []
        for hd in range(DN_HEADS):
            sl = slice(hd * HEAD_DIM, (hd + 1) * HEAD_DIM)
            qh = q_t[j:j + 1, sl]
            kh = k_t[j:j + 1, sl]
            vh = v_t[j:j + 1, sl]
            beta = beta_t[j:j + 1, hd:hd + 1]
            a = a_t[j:j + 1, DECAY_LANE + hd:DECAY_LANE + hd + 1]
            k_col = jnp.sum(eye * kh, axis=-1, keepdims=True)
            q_col = jnp.sum(eye * qh, axis=-1, keepdims=True)
            s_old = s_ref[j, hd]
            ks = jnp.sum(k_col * s_old, axis=0, keepdims=True)
            qs = jnp.sum(q_col * s_old, axis=0, keepdims=True)
            u = beta * vh - (beta * a) * ks
            qk = jnp.sum(qh * kh, axis=-1, keepdims=True)
            o_heads.append(a * qs + qk * u)
            snew_ref[j, hd] = a * s_old + k_col * u
        o_rows.append(jnp.concatenate(o_heads, axis=-1))
    o_scr[rows, :] = jnp.concatenate(o_rows, axis=0)

    @pl.when(i == n_i - 1)
    def _():
        o_dn = _gated_out_norm(o_scr[...], z_scr[...], onorm_ref[...])
        mix = _dot(jnp.concatenate([o_dn, opool_scr[...]], axis=-1), wout_ref[...])
        y_ref[...] = _layer_norm(DN_ALPHA * h_ref[...] + mix, g2_ref[...], b2_ref[...])


def _mixer_sample(h, s0, cbuf, pbuf, win, convw, alog, dtb, onorm, poolw, pscale, wout, g2, b2):
    bsz = h.shape[0]
    assert bsz % SAMPLE_TILE == 0
    cb2 = cbuf.reshape(bsz, (CONV_W - 1) * QKV_DIM)
    pb2 = pbuf.reshape(bsz, POOL_BUF * POOL_DIM)
    out_shape = (
        jax.ShapeDtypeStruct((bsz, D_MODEL), F32),
        jax.ShapeDtypeStruct(s0.shape, F32),
        jax.ShapeDtypeStruct(cb2.shape, F32),
        jax.ShapeDtypeStruct(pb2.shape, F32),
    )
    state_spec = pl.BlockSpec((SAMPLE_TILE, DN_HEADS, HEAD_DIM, HEAD_DIM), lambda i: (i, 0, 0, 0))
    y, s_new, c_new, p_new = pl.pallas_call(
        _mixer_sample_kernel,
        out_shape=out_shape,
        grid=(bsz // SAMPLE_TILE,),
        in_specs=[
            _resident((bsz, D_MODEL)),
            state_spec,
            _resident(cb2.shape),
            _resident(pb2.shape),
            _resident((D_MODEL, PROJ_DIM)),
            _resident((CONV_W, QKV_DIM)),
            _resident((1, GATE_PAD)),
            _resident((1, GATE_PAD)),
            _resident((1, HEAD_DIM)),
            _resident((len(POOL_WINDOWS), POOL_GROUP_DIM, POOL_GROUP_DIM)),
            _resident((1, POOL_DIM)),
            _resident((DN_DIM + POOL_DIM, D_MODEL)),
            _resident((1, D_MODEL)),
            _resident((1, D_MODEL)),
        ],
        out_specs=(
            pl.BlockSpec((bsz, D_MODEL), lambda i: (0, 0)),
            state_spec,
            pl.BlockSpec(cb2.shape, lambda i: (0, 0)),
            pl.BlockSpec(pb2.shape, lambda i: (0, 0)),
        ),
        scratch_shapes=[
            pltpu.VMEM((bsz, DN_DIM), F32),
            pltpu.VMEM((bsz, DN_DIM), F32),
            pltpu.VMEM((bsz, DN_DIM), F32),
            pltpu.VMEM((bsz, GATE_PAD), F32),
            pltpu.VMEM((bsz, GATE_PAD), F32),
            pltpu.VMEM((bsz, DN_DIM), F32),
            pltpu.VMEM((bsz, POOL_DIM), F32),
            pltpu.VMEM((bsz, DN_DIM), F32),
        ],
        compiler_params=pltpu.CompilerParams(dimension_semantics=("arbitrary",)),
        name="mixer_sample",
    )(h, s0, cb2, pb2, win, convw, alog, dtb, onorm, poolw, pscale, wout, g2, b2)
    return y, s_new, c_new.reshape(cbuf.shape), p_new.reshape(pbuf.shape)


def _pad_gate_param(v):
    out = jnp.zeros((v.shape[0], 1, GATE_PAD), F32)
    return out.at[:, 0, DECAY_LANE:DECAY_LANE + DN_HEADS].set(v.astype(F32))


def kernel(x_prompt, x_sample, state_delta, state_conv, state_pool, ln1_g, ln1_b, ffn1_w_gate, ffn1_w_up, ffn1_w_down, w_in, conv_w, a_log, dt_bias, onorm_g, pool_w, pool_scale, w_out, ln2_g, ln2_b, ffn2_w_gate, ffn2_w_up, ffn2_w_down, ln3_g, ln3_b):
    bsz, seq, _ = x_prompt.shape
    dec_b = x_sample.shape[0]

    b_off = QKV_DIM + DN_DIM
    p_off = b_off + 2 * DN_HEADS
    win = jnp.concatenate(
        [w_in[:, :, :b_off], w_in[:, :, p_off:], w_in[:, :, b_off:p_off],
         jnp.zeros((DEPTH, D_MODEL, GATE_PAD - 2 * DN_HEADS), w_in.dtype)], axis=-1).astype(BF16)
    wout = w_out.astype(BF16)
    poolw = pool_w.astype(BF16)
    f1g, f1u, f1d = ffn1_w_gate.astype(BF16), ffn1_w_up.astype(BF16), ffn1_w_down.astype(BF16)
    f2g, f2u, f2d = ffn2_w_gate.astype(BF16), ffn2_w_up.astype(BF16), ffn2_w_down.astype(BF16)
    alog = _pad_gate_param(a_log)
    dtb = _pad_gate_param(dt_bias)
    row = lambda v: v.reshape(DEPTH, 1, -1)
    ln1g, ln1b, ln2g, ln2b, ln3g, ln3b = map(row, (ln1_g, ln1_b, ln2_g, ln2_b, ln3_g, ln3_b))
    onorm = row(onorm_g)
    pscale = row(pool_scale)

    xp = x_prompt.reshape(bsz * seq, D_MODEL)
    xs = x_sample.reshape(dec_b, D_MODEL)
    dp, cp, pp, ds, cs, ps = [], [], [], [], [], []
    for l in range(DEPTH):
        mix_w = (win[l], conv_w[l], alog[l], dtb[l], onorm[l], poolw[l], pscale[l], wout[l], ln2g[l], ln2b[l])
        hp = _ffn_ln(xp, f1g[l], f1u[l], f1d[l], ln1g[l], ln1b[l])
        hs = _ffn_ln(xs, f1g[l], f1u[l], f1d[l], ln1g[l], ln1b[l])
        hp, s_p, c_p, p_p = _mixer_prompt(hp.reshape(bsz, seq, D_MODEL), *mix_w)
        hs, s_s, c_s, p_s = _mixer_sample(hs, state_delta[l], state_conv[l], state_pool[l], *mix_w)
        xp = _ffn_ln(hp.reshape(bsz * seq, D_MODEL), f2g[l], f2u[l], f2d[l], ln3g[l], ln3b[l])
        xs = _ffn_ln(hs, f2g[l], f2u[l], f2d[l], ln3g[l], ln3b[l])
        dp.append(s_p); cp.append(c_p); pp.append(p_p)
        ds.append(s_s); cs.append(c_s); ps.append(p_s)
    return (xp.reshape(bsz, seq, D_MODEL), xs.reshape(dec_b, 1, D_MODEL),
            jnp.stack(dp), jnp.stack(cp), jnp.stack(pp), jnp.stack(ds), jnp.stack(cs), jnp.stack(ps))
```

```python
import functools

import jax
import jax.numpy as jnp
from jax import lax
from jax.experimental import pallas as pl
from jax.experimental.pallas import tpu as pltpu

F32 = jnp.float32
BF16 = jnp.bfloat16

D_MODEL = 1024
DEPTH = 4
DN_HEADS = 4
HEAD_DIM = 128
DN_DIM = DN_HEADS * HEAD_DIM
POOL_DIM = 512
QKV_DIM = 3 * DN_DIM
CONV_W = 4
POOL_WINDOWS = (2, 4, 8, 16)
POOL_GROUP_DIM = POOL_DIM // len(POOL_WINDOWS)
POOL_BUF = max(POOL_WINDOWS) - 1
D_FF = 2816
PAST_LEN = 16384
DN_ALPHA = (2.0 * DEPTH) ** 0.25
LN_EPS = 1e-5
RMS_EPS = 1e-6
L2_EPS = 1e-6

GATE_PAD = 128
PROJ_DIM = QKV_DIM + DN_DIM + POOL_DIM + GATE_PAD
Z_OFF = QKV_DIM
P_OFF = QKV_DIM + DN_DIM
G_OFF = QKV_DIM + DN_DIM + POOL_DIM
DECAY_LANE = DN_HEADS

CHUNK = 128
INV_BASE = 8
FFN_ROWS = 1024
FFN_SUB_ROWS = 256
MIX_ROWS = 512
SAMPLE_TILE = 8


def _dot(a, b):
    return jnp.dot(a.astype(BF16), b.astype(BF16), preferred_element_type=F32)


def _dot_nt(a, b):
    return lax.dot_general(a.astype(BF16), b.astype(BF16), (((1,), (1,)), ((), ())),
                           preferred_element_type=F32)


def _dot_tn(a, b):
    return lax.dot_general(a.astype(BF16), b.astype(BF16), (((0,), (0,)), ((), ())),
                           preferred_element_type=F32)


def _dot_exact_lhs(a_bf16, b):
    b1 = b.astype(BF16)
    r1 = b - b1.astype(F32)
    b2 = r1.astype(BF16)
    b3 = (r1 - b2.astype(F32)).astype(BF16)
    d = functools.partial(jnp.dot, preferred_element_type=F32)
    return d(a_bf16, b1) + (d(a_bf16, b2) + d(a_bf16, b3))


def _silu(x):
    return x * jax.nn.sigmoid(x)


def _softplus(x):
    return jnp.maximum(x, 0.0) + jnp.log1p(jnp.exp(-jnp.abs(x)))


def _layer_norm(y, g, b):
    mu = jnp.mean(y, axis=-1, keepdims=True)
    yc = y - mu
    var = jnp.mean(yc * yc, axis=-1, keepdims=True)
    return yc * lax.rsqrt(var + LN_EPS) * g + b


def _l2norm(x):
    return x * lax.rsqrt(jnp.sum(x * x, axis=-1, keepdims=True) + L2_EPS)


def _ffn_ln_kernel(x_ref, wg_ref, wu_ref, wd_ref, g_ref, b_ref, o_ref):
    tm = x_ref.shape[0]
    sub = min(FFN_SUB_ROWS, tm)
    for r in range(tm // sub):
        rows = slice(r * sub, (r + 1) * sub)
        x = x_ref[rows, :]
        xb = x.astype(BF16)
        gate = jnp.dot(xb, wg_ref[...], preferred_element_type=F32)
        up = jnp.dot(xb, wu_ref[...], preferred_element_type=F32)
        act = (_silu(gate) * up).astype(BF16)
        ff = jnp.dot(act, wd_ref[...], preferred_element_type=F32)
        o_ref[rows, :] = _layer_norm(DN_ALPHA * x + 0.5 * ff, g_ref[...], b_ref[...])


def _resident(shape):
    zeros = (0,) * len(shape)
    return pl.BlockSpec(shape, lambda *_: zeros, pipeline_mode=pl.Buffered(1))


def _layer_resident(layer, shape):
    index = (layer,) + (0,) * len(shape)
    return pl.BlockSpec((None,) + tuple(shape), lambda *_: index, pipeline_mode=pl.Buffered(1))


def _ffn_ln(layer, x, wg, wu, wd, g, b):
    rows = x.shape[0]
    tm = min(FFN_ROWS, rows)
    assert rows % tm == 0
    return pl.pallas_call(
        _ffn_ln_kernel,
        out_shape=jax.ShapeDtypeStruct((rows, D_MODEL), F32),
        grid=(rows // tm,),
        in_specs=[
            pl.BlockSpec((tm, D_MODEL), lambda i: (i, 0)),
            _layer_resident(layer, (D_MODEL, D_FF)),
            _layer_resident(layer, (D_MODEL, D_FF)),
            _layer_resident(layer, (D_FF, D_MODEL)),
            _layer_resident(layer, (1, D_MODEL)),
            _layer_resident(layer, (1, D_MODEL)),
        ],
        out_specs=pl.BlockSpec((tm, D_MODEL), lambda i: (i, 0)),
        compiler_params=pltpu.CompilerParams(dimension_semantics=("arbitrary",)),
        name="ffn_ln",
    )(x, wg, wu, wd, g, b)


def _gates(ba, alog, dtb):
    beta = jax.nn.sigmoid(ba)
    g = -jnp.exp(alog) * _softplus(ba + dtb)
    return beta, g


def _qkv_heads(conv_out):
    act = _silu(conv_out)
    qs, ks = [], []
    for h in range(DN_HEADS):
        lo = h * HEAD_DIM
        qs.append(_l2norm(act[:, lo:lo + HEAD_DIM]) * (HEAD_DIM ** -0.5))
        ks.append(_l2norm(act[:, DN_DIM + lo:DN_DIM + lo + HEAD_DIM]))
    return jnp.concatenate(qs, -1), jnp.concatenate(ks, -1), act[:, 2 * DN_DIM:]


def _gated_out_norm(o, z, onorm_g):
    outs = []
    for h in range(DN_HEADS):
        sl = slice(h * HEAD_DIM, (h + 1) * HEAD_DIM)
        oh = o[:, sl]
        oh = oh * lax.rsqrt(jnp.mean(oh * oh, axis=-1, keepdims=True) + RMS_EPS)
        outs.append(oh * onorm_g * _silu(z[:, sl]))
    return jnp.concatenate(outs, -1)


def _pool_project(d_groups, poolw_ref, pscale):
    ys = [_dot(d, poolw_ref[gi]) for gi, d in enumerate(d_groups)]
    return jnp.concatenate(ys, -1) * pscale


def _inv_unit_lower(ms, row, col, size):
    eye = jnp.where(row == col, 1.0, 0.0).astype(F32)
    base_bits = INV_BASE.bit_length() - 1
    base_mask = (row >> base_bits) == (col >> base_bits)
    ps = [jnp.where(base_mask, m, 0.0) for m in ms]
    xs = [eye - p for p in ps]
    for _ in range(base_bits - 1):
        ps = [_dot(p, p) for p in ps]
        xs = [x + _dot(x, p) for x, p in zip(xs, ps)]
    bits = base_bits
    while (1 << bits) < size:
        same_pair = (row >> (bits + 1)) == (col >> (bits + 1))
        lower_left = (((row >> bits) & 1) == 1) & (((col >> bits) & 1) == 0)
        mask = same_pair & lower_left
        ys = [_dot(jnp.where(mask, m, 0.0), x) for m, x in zip(ms, xs)]
        xs = [x - _dot(x, y) for x, y in zip(xs, ys)]
        bits += 1
    return xs


def _mixer_prompt_kernel(h_ref, win_ref, convw_ref, alog_ref, dtb_ref, onorm_ref, poolw_ref, pscale_ref,
                         wout_ref, g2_ref, b2_ref,
                         y_ref, sfin_ref, cfin_ref, pfin_ref,
                         s_scr, proj_scr, ext_scr, pext_scr, qkv_scr, beta_scr, g_scr, o_scr):
    tl = h_ref.shape[0]
    l = pl.program_id(1)
    n_l = pl.num_programs(1)

    @pl.when(l == 0)
    def _():
        s_scr[...] = jnp.zeros_like(s_scr)
        ext_scr[:, 0:8, :] = jnp.zeros((QKV_DIM // HEAD_DIM, 8, HEAD_DIM), F32)
        pext_scr[:, 0:16, :] = jnp.zeros((len(POOL_WINDOWS), 16, POOL_GROUP_DIM), F32)

    h = h_ref[...]
    proj_scr[...] = jnp.dot(h.astype(BF16), win_ref[...], preferred_element_type=F32)

    for t in range(QKV_DIM // HEAD_DIM):
        lanes = slice(t * HEAD_DIM, (t + 1) * HEAD_DIM)
        x_t = proj_scr[:, lanes]
        ext_scr[t, 8:8 + tl, :] = x_t
        conv = convw_ref[CONV_W - 1:CONV_W, lanes] * x_t
        for i in range(CONV_W - 1):
            conv = conv + convw_ref[i:i + 1, lanes] * ext_scr[t, 5 + i:5 + i + tl, :]
        cfin_ref[:, lanes] = ext_scr[t, tl + 5:tl + 8, :]
        ext_scr[t, 0:8, :] = ext_scr[t, tl:tl + 8, :]
        act = _silu(conv)
        if t < DN_HEADS:
            act = _l2norm(act) * (HEAD_DIM ** -0.5)
        elif t < 2 * DN_HEADS:
            act = _l2norm(act)
        qkv_scr[t] = act

    beta_t, g_t = _gates(proj_scr[:, G_OFF:G_OFF + GATE_PAD], alog_ref[...], dtb_ref[...])
    beta_scr[...] = beta_t
    g_scr[...] = g_t

    pos = l * tl + lax.broadcasted_iota(jnp.int32, (tl, 1), 0)
    d_groups = []
    for gi, w in enumerate(POOL_WINDOWS):
        lanes = slice(gi * POOL_GROUP_DIM, (gi + 1) * POOL_GROUP_DIM)
        tok = proj_scr[:, P_OFF + gi * POOL_GROUP_DIM:P_OFF + (gi + 1) * POOL_GROUP_DIM]
        pext_scr[gi, 16:16 + tl, :] = tok
        s = tok
        for i in range(1, w):
            s = s + pext_scr[gi, 16 - i:16 - i + tl, :]
        cnt = jnp.minimum(w, pos + 1).astype(F32)
        d_groups.append(s / cnt - tok)
        pfin_ref[:, lanes] = pext_scr[gi, tl + 1:tl + 16, :]
        pext_scr[gi, 0:16, :] = pext_scr[gi, tl:tl + 16, :]
    o_pool = _pool_project(d_groups, poolw_ref, pscale_ref[...])

    c = CHUNK
    row = lax.broadcasted_iota(jnp.int32, (c, c), 0)
    col = lax.broadcasted_iota(jnp.int32, (c, c), 1)
    incl = row >= col
    strict = row > col
    tri = jnp.where(incl, 1.0, 0.0).astype(BF16)

    n_chunks = tl // c
    chains = [(ci, hd) for ci in range(n_chunks) for hd in range(DN_HEADS)]
    gc_alls, gc_ts, beta_alls = [], [], []
    for ci in range(n_chunks):
        rows = slice(ci * c, (ci + 1) * c)
        gc_all = _dot_exact_lhs(tri, g_scr[rows, :])
        gc_alls.append(gc_all)
        gc_ts.append(gc_all.T)
        beta_alls.append(beta_scr[rows, :])
    qs, ks, betas, egs, decs, rhss, k_ends, g_ends = [], [], [], [], [], [], [], []
    for ci, hd in chains:
        rows = slice(ci * c, (ci + 1) * c)
        sl = slice(hd * HEAD_DIM, (hd + 1) * HEAD_DIM)
        lane = DECAY_LANE + hd
        qh, kh, vh = qkv_scr[hd, rows, :], qkv_scr[DN_HEADS + hd, rows, :], qkv_scr[2 * DN_HEADS + hd, rows, :]
        beta = beta_alls[ci][:, hd:hd + 1]
        gc_col = gc_alls[ci][:, lane:lane + 1]
        gc_row = gc_ts[ci][lane:lane + 1, :]
        gc_last = gc_alls[ci][c - 1:c, lane:lane + 1]
        eg = jnp.exp(gc_col)
        decs.append(jnp.where(incl, jnp.exp(jnp.where(incl, gc_col - gc_row, 0.0)), 0.0))
        rhss.append(jnp.concatenate([beta * vh, (beta * eg) * kh], axis=-1))
        k_ends.append(kh * jnp.exp(gc_last - gc_col))
        g_ends.append(jnp.exp(gc_last))
        qs.append(qh); ks.append(kh); betas.append(beta); egs.append(eg)
    kqs = [_dot_nt(jnp.concatenate([kh, qh], axis=0), kh) for kh, qh in zip(ks, qs)]
    ms = [jnp.where(strict, beta * kq[:c] * dec, 0.0) for beta, kq, dec in zip(betas, kqs, decs)]
    qks = [kq[c:] * dec for kq, dec in zip(kqs, decs)]
    t_invs = _inv_unit_lower(ms, row, col, c)
    sols = [_dot(t_inv, rhs) for t_inv, rhs in zip(t_invs, rhss)]
    q_decs = [qh * eg for qh, eg in zip(qs, egs)]

    s_cur = [s_scr[hd] for hd in range(DN_HEADS)]
    for ci in range(n_chunks):
        idx = [ci * DN_HEADS + hd for hd in range(DN_HEADS)]
        wss = [_dot(jnp.concatenate([sols[i][:, HEAD_DIM:], q_decs[i]], axis=0), s) for i, s in zip(idx, s_cur)]
        us = [sols[i][:, :HEAD_DIM] - ws[:c] for i, ws in zip(idx, wss)]
        outs = [ws[c:] + _dot(qks[i], u) for i, ws, u in zip(idx, wss, us)]
        s_cur = [g_ends[i] * s + _dot_tn(k_ends[i], u) for i, s, u in zip(idx, s_cur, us)]
        o_scr[ci * c:(ci + 1) * c, :] = jnp.concatenate(outs, axis=-1)
    for hd in range(DN_HEADS):
        s_scr[hd] = s_cur[hd]

    @pl.when(l == n_l - 1)
    def _():
        sfin_ref[...] = s_scr[...]

    o_dn = _gated_out_norm(o_scr[...], proj_scr[:, Z_OFF:Z_OFF + DN_DIM], onorm_ref[...])
    mix = _dot(jnp.concatenate([o_dn, o_pool], axis=-1), wout_ref[...])
    y_ref[...] = _layer_norm(DN_ALPHA * h + mix, g2_ref[...], b2_ref[...])


def _mixer_weight_specs(layer):
    return [
        _layer_resident(layer, (D_MODEL, PROJ_DIM)),
        _layer_resident(layer, (CONV_W, QKV_DIM)),
        _layer_resident(layer, (1, GATE_PAD)),
        _layer_resident(layer, (1, GATE_PAD)),
        _layer_resident(layer, (1, HEAD_DIM)),
        _layer_resident(layer, (len(POOL_WINDOWS), POOL_GROUP_DIM, POOL_GROUP_DIM)),
        _layer_resident(layer, (1, POOL_DIM)),
        _layer_resident(layer, (DN_DIM + POOL_DIM, D_MODEL)),
        _layer_resident(layer, (1, D_MODEL)),
        _layer_resident(layer, (1, D_MODEL)),
    ]


def _mixer_prompt(layer, h, *weights):
    bsz, seq, _ = h.shape
    tl = MIX_ROWS
    assert seq % tl == 0 and tl % CHUNK == 0
    out_shape = (
        jax.ShapeDtypeStruct((bsz, seq, D_MODEL), F32),
        jax.ShapeDtypeStruct((bsz, DN_HEADS, HEAD_DIM, HEAD_DIM), F32),
        jax.ShapeDtypeStruct((bsz, CONV_W - 1, QKV_DIM), F32),
        jax.ShapeDtypeStruct((bsz, POOL_BUF, POOL_DIM), F32),
    )
    return pl.pallas_call(
        _mixer_prompt_kernel,
        out_shape=out_shape,
        grid=(bsz, seq // tl),
        in_specs=[
            pl.BlockSpec((None, tl, D_MODEL), lambda b, l: (b, l, 0)),
        ] + _mixer_weight_specs(layer),
        out_specs=(
            pl.BlockSpec((None, tl, D_MODEL), lambda b, l: (b, l, 0)),
            pl.BlockSpec((None, DN_HEADS, HEAD_DIM, HEAD_DIM), lambda b, l: (b, 0, 0, 0)),
            pl.BlockSpec((None, CONV_W - 1, QKV_DIM), lambda b, l: (b, 0, 0)),
            pl.BlockSpec((None, POOL_BUF, POOL_DIM), lambda b, l: (b, 0, 0)),
        ),
        scratch_shapes=[
            pltpu.VMEM((DN_HEADS, HEAD_DIM, HEAD_DIM), F32),
            pltpu.VMEM((tl, PROJ_DIM), F32),
            pltpu.VMEM((QKV_DIM // HEAD_DIM, tl + 8, HEAD_DIM), F32),
            pltpu.VMEM((len(POOL_WINDOWS), tl + 16, POOL_GROUP_DIM), F32),
            pltpu.VMEM((QKV_DIM // HEAD_DIM, tl, HEAD_DIM), F32),
            pltpu.VMEM((tl, GATE_PAD), F32),
            pltpu.VMEM((tl, GATE_PAD), F32),
            pltpu.VMEM((tl, DN_DIM), F32),
        ],
        compiler_params=pltpu.CompilerParams(dimension_semantics=("arbitrary", "arbitrary")),
        name="mixer_prompt",
    )(h, *weights)


def _mixer_sample_kernel(h_ref, s_ref, cbuf_ref, pbuf_ref, win_ref, convw_ref, alog_ref, dtb_ref, onorm_ref,
                         poolw_ref, pscale_ref, wout_ref, g2_ref, b2_ref, sprev_ref,
                         y_ref, snew_ref, cnew_ref, pnew_ref,
                         q_scr, k_scr, v_scr, beta_scr, eg_scr, z_scr, opool_scr, o_scr):
    del sprev_ref
    i = pl.program_id(0)
    n_i = pl.num_programs(0)

    @pl.when(i == 0)
    def _():
        proj = jnp.dot(h_ref[...].astype(BF16), win_ref[...], preferred_element_type=F32)
        qkv = proj[:, 0:QKV_DIM]
        conv = convw_ref[CONV_W - 1:CONV_W, :] * qkv
        for j in range(CONV_W - 1):
            conv = conv + convw_ref[j:j + 1, :] * cbuf_ref[:, j, :]
        for j in range(CONV_W - 2):
            cnew_ref[:, j, :] = cbuf_ref[:, j + 1, :]
        cnew_ref[:, CONV_W - 2, :] = qkv
        q, k, v = _qkv_heads(conv)
        q_scr[...] = q
        k_scr[...] = k
        v_scr[...] = v
        beta_t, g_t = _gates(proj[:, G_OFF:G_OFF + GATE_PAD], alog_ref[...], dtb_ref[...])
        beta_scr[...] = beta_t
        eg_scr[...] = jnp.exp(g_t)
        z_scr[...] = proj[:, Z_OFF:Z_OFF + DN_DIM]

        p = proj[:, P_OFF:P_OFF + POOL_DIM]
        d_groups = []
        for gi, w in enumerate(POOL_WINDOWS):
            lo = gi * POOL_GROUP_DIM
            tok = p[:, lo:lo + POOL_GROUP_DIM]
            s = tok
            for back in range(1, w):
                s = s + pbuf_ref[:, POOL_BUF - back, lo:lo + POOL_GROUP_DIM]
            cnt = float(min(w, PAST_LEN + 1))
            d_groups.append(s / cnt - tok)
        opool_scr[...] = _pool_project(d_groups, poolw_ref, pscale_ref[...])
        for j in range(POOL_BUF - 1):
            pnew_ref[:, j, :] = pbuf_ref[:, j + 1, :]
        pnew_ref[:, POOL_BUF - 1, :] = p

    eye = jnp.where(lax.broadcasted_iota(jnp.int32, (HEAD_DIM, HEAD_DIM), 0)
                    == lax.broadcasted_iota(jnp.int32, (HEAD_DIM, HEAD_DIM), 1), 1.0, 0.0).astype(F32)

    rows = pl.ds(pl.multiple_of(i * SAMPLE_TILE, SAMPLE_TILE), SAMPLE_TILE)
    q_t, k_t, v_t = q_scr[rows, :], k_scr[rows, :], v_scr[rows, :]
    beta_t, a_t = beta_scr[rows, :], eg_scr[rows, :]
    o_rows = []
    for j in range(SAMPLE_TILE):
        o_heads = []
        for hd in range(DN_HEADS):
            sl = slice(hd * HEAD_DIM, (hd + 1) * HEAD_DIM)
            qh = q_t[j:j + 1, sl]
            kh = k_t[j:j + 1, sl]
            vh = v_t[j:j + 1, sl]
            beta = beta_t[j:j + 1, hd:hd + 1]
            a = a_t[j:j + 1, DECAY_LANE + hd:DECAY_LANE + hd + 1]
            k_col = jnp.sum(eye * kh, axis=-1, keepdims=True)
            q_col = jnp.sum(eye * qh, axis=-1, keepdims=True)
            s_old = s_ref[j, hd]
            ks = jnp.sum(k_col * s_old, axis=0, keepdims=True)
            qs = jnp.sum(q_col * s_old, axis=0, keepdims=True)
            u = beta * vh - (beta * a) * ks
            qk = jnp.sum(qh * kh, axis=-1, keepdims=True)
            o_heads.append(a * qs + qk * u)
            snew_ref[j, hd] = a * s_old + k_col * u
        o_rows.append(jnp.concatenate(o_heads, axis=-1))
    o_scr[rows, :] = jnp.concatenate(o_rows, axis=0)

    @pl.when(i == n_i - 1)
    def _():
        o_dn = _gated_out_norm(o_scr[...], z_scr[...], onorm_ref[...])
        mix = _dot(jnp.concatenate([o_dn, opool_scr[...]], axis=-1), wout_ref[...])
        y_ref[...] = _layer_norm(DN_ALPHA * h_ref[...] + mix, g2_ref[...], b2_ref[...])


def _mixer_sample(layer, h, s_all, cbuf_all, pbuf_all, s_out_prev, *weights):
    bsz = h.shape[0]
    assert bsz % SAMPLE_TILE == 0
    cshape = (bsz, CONV_W - 1, QKV_DIM)
    pshape = (bsz, POOL_BUF, POOL_DIM)
    out_shape = (
        jax.ShapeDtypeStruct((bsz, D_MODEL), F32),
        jax.ShapeDtypeStruct(s_all.shape, F32),
        jax.ShapeDtypeStruct(cshape, F32),
        jax.ShapeDtypeStruct(pshape, F32),
    )
    state_spec = pl.BlockSpec((None, SAMPLE_TILE, DN_HEADS, HEAD_DIM, HEAD_DIM), lambda i: (layer, i, 0, 0, 0))
    n_in = 4 + 10
    return pl.pallas_call(
        _mixer_sample_kernel,
        out_shape=out_shape,
        grid=(bsz // SAMPLE_TILE,),
        in_specs=[
            _resident((bsz, D_MODEL)),
            state_spec,
            _layer_resident(layer, cshape),
            _layer_resident(layer, pshape),
        ] + _mixer_weight_specs(layer) + [pl.BlockSpec(memory_space=pl.ANY)],
        out_specs=(
            pl.BlockSpec((bsz, D_MODEL), lambda i: (0, 0)),
            state_spec,
            pl.BlockSpec(cshape, lambda i: (0, 0, 0)),
            pl.BlockSpec(pshape, lambda i: (0, 0, 0)),
        ),
        input_output_aliases={n_in: 1},
        scratch_shapes=[
            pltpu.VMEM((bsz, DN_DIM), F32),
            pltpu.VMEM((bsz, DN_DIM), F32),
            pltpu.VMEM((bsz, DN_DIM), F32),
            pltpu.VMEM((bsz, GATE_PAD), F32),
            pltpu.VMEM((bsz, GATE_PAD), F32),
            pltpu.VMEM((bsz, DN_DIM), F32),
            pltpu.VMEM((bsz, POOL_DIM), F32),
            pltpu.VMEM((bsz, DN_DIM), F32),
        ],
        compiler_params=pltpu.CompilerParams(dimension_semantics=("arbitrary",)),
        name="mixer_sample",
    )(h, s_all, cbuf_all, pbuf_all, *weights, s_out_prev)


def _pad_gate_param(v):
    out = jnp.zeros((v.shape[0], 1, GATE_PAD), F32)
    return out.at[:, 0, DECAY_LANE:DECAY_LANE + DN_HEADS].set(v.astype(F32))


def kernel(x_prompt, x_sample, state_delta, state_conv, state_pool, ln1_g, ln1_b, ffn1_w_gate, ffn1_w_up, ffn1_w_down, w_in, conv_w, a_log, dt_bias, onorm_g, pool_w, pool_scale, w_out, ln2_g, ln2_b, ffn2_w_gate, ffn2_w_up, ffn2_w_down, ln3_g, ln3_b):
    bsz, seq, _ = x_prompt.shape
    dec_b = x_sample.shape[0]

    b_off = QKV_DIM + DN_DIM
    p_off = b_off + 2 * DN_HEADS
    win = jnp.concatenate(
        [w_in[:, :, :b_off], w_in[:, :, p_off:], w_in[:, :, b_off:p_off],
         jnp.zeros((DEPTH, D_MODEL, GATE_PAD - 2 * DN_HEADS), w_in.dtype)], axis=-1).astype(BF16)
    wout = w_out.astype(BF16)
    poolw = pool_w.astype(BF16)
    f1g, f1u, f1d = ffn1_w_gate.astype(BF16), ffn1_w_up.astype(BF16), ffn1_w_down.astype(BF16)
    f2g, f2u, f2d = ffn2_w_gate.astype(BF16), ffn2_w_up.astype(BF16), ffn2_w_down.astype(BF16)
    alog = _pad_gate_param(a_log)
    dtb = _pad_gate_param(dt_bias)
    row = lambda v: v.reshape(DEPTH, 1, -1)
    ln1g, ln1b, ln2g, ln2b, ln3g, ln3b = map(row, (ln1_g, ln1_b, ln2_g, ln2_b, ln3_g, ln3_b))
    onorm = row(onorm_g)
    pscale = row(pool_scale)

    mix_w = (win, conv_w, alog, dtb, onorm, poolw, pscale, wout, ln2g, ln2b)
    xp = x_prompt.reshape(bsz * seq, D_MODEL)
    xs = x_sample.reshape(dec_b, D_MODEL)
    dp, cp, pp, cs, ps = [], [], [], [], []
    delta_sample = jnp.zeros(state_delta.shape, F32)
    for l in range(DEPTH):
        hp = _ffn_ln(l, xp, f1g, f1u, f1d, ln1g, ln1b)
        hs = _ffn_ln(l, xs, f1g, f1u, f1d, ln1g, ln1b)
        hp, s_p, c_p, p_p = _mixer_prompt(l, hp.reshape(bsz, seq, D_MODEL), *mix_w)
        hs, delta_sample, c_s, p_s = _mixer_sample(l, hs, state_delta, state_conv, state_pool, delta_sample, *mix_w)
        xp = _ffn_ln(l, hp.reshape(bsz * seq, D_MODEL), f2g, f2u, f2d, ln3g, ln3b)
        xs = _ffn_ln(l, hs, f2g, f2u, f2d, ln3g, ln3b)
        dp.append(s_p); cp.append(c_p); pp.append(p_p)
        cs.append(c_s); ps.append(p_s)
    return (xp.reshape(bsz, seq, D_MODEL), xs.reshape(dec_b, 1, D_MODEL),
            jnp.stack(dp), jnp.stack(cp), jnp.stack(pp), delta_sample, jnp.stack(cs), jnp.stack(ps))
```

```python
import functools

import jax
import jax.numpy as jnp
from jax import lax
from jax.experimental import pallas as pl
from jax.experimental.pallas import tpu as pltpu

F32 = jnp.float32
BF16 = jnp.bfloat16

D_MODEL = 1024
DEPTH = 4
DN_HEADS = 4
HEAD_DIM = 128
DN_DIM = DN_HEADS * HEAD_DIM
POOL_DIM = 512
QKV_DIM = 3 * DN_DIM
CONV_W = 4
POOL_WINDOWS = (2, 4, 8, 16)
POOL_GROUP_DIM = POOL_DIM // len(POOL_WINDOWS)
POOL_BUF = max(POOL_WINDOWS) - 1
D_FF = 2816
PAST_LEN = 16384
DN_ALPHA = (2.0 * DEPTH) ** 0.25
LN_EPS = 1e-5
RMS_EPS = 1e-6
L2_EPS = 1e-6

GATE_PAD = 128
PROJ_DIM = QKV_DIM + DN_DIM + POOL_DIM + GATE_PAD
Z_OFF = QKV_DIM
P_OFF = QKV_DIM + DN_DIM
G_OFF = QKV_DIM + DN_DIM + POOL_DIM
DECAY_LANE = DN_HEADS

CHUNK = 128
INV_BASE = 8
FFN_ROWS = 1024
FFN_SUB_ROWS = 256
MIX_ROWS = 512
OUT_CHUNKS = 2
SAMPLE_TILE = 8


def _dot(a, b):
    return jnp.dot(a.astype(BF16), b.astype(BF16), preferred_element_type=F32)


def _dot_nt(a, b):
    return lax.dot_general(a.astype(BF16), b.astype(BF16), (((1,), (1,)), ((), ())),
                           preferred_element_type=F32)


def _dot_tn(a, b):
    return lax.dot_general(a.astype(BF16), b.astype(BF16), (((0,), (0,)), ((), ())),
                           preferred_element_type=F32)


def _dot_exact_lhs(a_bf16, b):
    b1 = b.astype(BF16)
    r1 = b - b1.astype(F32)
    b2 = r1.astype(BF16)
    b3 = (r1 - b2.astype(F32)).astype(BF16)
    d = functools.partial(jnp.dot, preferred_element_type=F32)
    return d(a_bf16, b1) + (d(a_bf16, b2) + d(a_bf16, b3))


def _silu(x):
    return x * jax.nn.sigmoid(x)


def _softplus(x):
    return jnp.maximum(x, 0.0) + jnp.log1p(jnp.exp(-jnp.abs(x)))


def _layer_norm(y, g, b):
    mu = jnp.mean(y, axis=-1, keepdims=True)
    yc = y - mu
    var = jnp.mean(yc * yc, axis=-1, keepdims=True)
    return yc * lax.rsqrt(var + LN_EPS) * g + b


def _l2norm(x):
    return x * lax.rsqrt(jnp.sum(x * x, axis=-1, keepdims=True) + L2_EPS)


def _ffn_ln_rows(x, wg_ref, wu_ref, wd_ref, g_ref, b_ref):
    xb = x.astype(BF16)
    gate = jnp.dot(xb, wg_ref[...], preferred_element_type=F32)
    up = jnp.dot(xb, wu_ref[...], preferred_element_type=F32)
    act = (_silu(gate) * up).astype(BF16)
    ff = jnp.dot(act, wd_ref[...], preferred_element_type=F32)
    return _layer_norm(DN_ALPHA * x + 0.5 * ff, g_ref[...], b_ref[...])


def _ffn_ln_kernel(x_ref, xs_ref, wg_ref, wu_ref, wd_ref, g_ref, b_ref, o_ref, os_ref):
    weights = (wg_ref, wu_ref, wd_ref, g_ref, b_ref)
    tm = x_ref.shape[0]
    for r in range(tm // FFN_SUB_ROWS):
        rows = slice(r * FFN_SUB_ROWS, (r + 1) * FFN_SUB_ROWS)
        o_ref[rows, :] = _ffn_ln_rows(x_ref[rows, :], *weights)

    @pl.when(pl.program_id(0) == pl.num_programs(0) - 1)
    def _():
        os_ref[...] = _ffn_ln_rows(xs_ref[...], *weights)


def _resident(shape):
    zeros = (0,) * len(shape)
    return pl.BlockSpec(shape, lambda *_: zeros, pipeline_mode=pl.Buffered(1))


def _layer_resident(layer, shape):
    index = (layer,) + (0,) * len(shape)
    return pl.BlockSpec((None,) + tuple(shape), lambda *_: index, pipeline_mode=pl.Buffered(1))


def _ffn_ln(layer, x, xs, wg, wu, wd, g, b):
    rows = x.shape[0]
    tm = FFN_ROWS
    assert rows % tm == 0 and tm % FFN_SUB_ROWS == 0
    return pl.pallas_call(
        _ffn_ln_kernel,
        out_shape=(jax.ShapeDtypeStruct((rows, D_MODEL), F32), jax.ShapeDtypeStruct(xs.shape, F32)),
        grid=(rows // tm,),
        in_specs=[
            pl.BlockSpec((tm, D_MODEL), lambda i: (i, 0)),
            _resident(xs.shape),
            _layer_resident(layer, (D_MODEL, D_FF)),
            _layer_resident(layer, (D_MODEL, D_FF)),
            _layer_resident(layer, (D_FF, D_MODEL)),
            _layer_resident(layer, (1, D_MODEL)),
            _layer_resident(layer, (1, D_MODEL)),
        ],
        out_specs=(pl.BlockSpec((tm, D_MODEL), lambda i: (i, 0)),
                   pl.BlockSpec(xs.shape, lambda i: (0, 0))),
        compiler_params=pltpu.CompilerParams(dimension_semantics=("arbitrary",)),
        name="ffn_ln",
    )(x, xs, wg, wu, wd, g, b)


def _gates(ba, alog, dtb):
    beta = jax.nn.sigmoid(ba)
    g = -jnp.exp(alog) * _softplus(ba + dtb)
    return beta, g


def _qkv_heads(conv_out):
    act = _silu(conv_out)
    qs, ks = [], []
    for h in range(DN_HEADS):
        lo = h * HEAD_DIM
        qs.append(_l2norm(act[:, lo:lo + HEAD_DIM]) * (HEAD_DIM ** -0.5))
        ks.append(_l2norm(act[:, DN_DIM + lo:DN_DIM + lo + HEAD_DIM]))
    return jnp.concatenate(qs, -1), jnp.concatenate(ks, -1), act[:, 2 * DN_DIM:]


def _gated_out_norm(o, z, onorm_g):
    outs = []
    for h in range(DN_HEADS):
        sl = slice(h * HEAD_DIM, (h + 1) * HEAD_DIM)
        oh = o[:, sl]
        oh = oh * lax.rsqrt(jnp.mean(oh * oh, axis=-1, keepdims=True) + RMS_EPS)
        outs.append(oh * onorm_g * _silu(z[:, sl]))
    return jnp.concatenate(outs, -1)


def _pool_project(d_groups, poolw_ref, pscale):
    ys = [_dot(d, poolw_ref[gi]) for gi, d in enumerate(d_groups)]
    return jnp.concatenate(ys, -1) * pscale


def _inv_unit_lower(ms, row, col, size):
    eye = jnp.where(row == col, 1.0, 0.0).astype(F32)
    base_bits = INV_BASE.bit_length() - 1
    base_mask = (row >> base_bits) == (col >> base_bits)
    ps = [jnp.where(base_mask, m, 0.0) for m in ms]
    xs = [eye - p for p in ps]
    ps = [_dot(p, p) for p in ps]
    for step in range(base_bits - 1):
        if step < base_bits - 2:
            xps = [_dot(jnp.concatenate([x, p], axis=0), p) for x, p in zip(xs, ps)]
            xs = [x + xp[:size] for x, xp in zip(xs, xps)]
            ps = [xp[size:] for xp in xps]
        else:
            xs = [x + _dot(x, p) for x, p in zip(xs, ps)]
    bits = base_bits
    while (1 << bits) < size:
        same_pair = (row >> (bits + 1)) == (col >> (bits + 1))
        lower_left = (((row >> bits) & 1) == 1) & (((col >> bits) & 1) == 0)
        mask = same_pair & lower_left
        ys = [_dot(jnp.where(mask, m, 0.0), x) for m, x in zip(ms, xs)]
        xs = [x - _dot(x, y) for x, y in zip(xs, ys)]
        bits += 1
    return xs


def _mixer_prompt_kernel(h_ref, win_ref, convw_ref, alog_ref, dtb_ref, onorm_ref, poolw_ref, pscale_ref,
                         wout_ref, g2_ref, b2_ref,
                         y_ref, sfin_ref, cfin_ref, pfin_ref,
                         s_scr, ext_scr, pext_scr, qkv_scr, beta_scr, g_scr, z_scr, opool_scr):
    tl = h_ref.shape[0]
    l = pl.program_id(1)
    n_l = pl.num_programs(1)

    @pl.when(l == 0)
    def _():
        s_scr[...] = jnp.zeros_like(s_scr)
        ext_scr[:, 0:8, :] = jnp.zeros((QKV_DIM // HEAD_DIM, 8, HEAD_DIM), F32)
        pext_scr[:, 0:16, :] = jnp.zeros((len(POOL_WINDOWS), 16, POOL_GROUP_DIM), F32)

    hb = h_ref[...].astype(BF16)

    def project(lo, width):
        return jnp.dot(hb, win_ref[:, lo:lo + width], preferred_element_type=F32)

    def conv_slab(t, x_t):
        lanes = slice(t * HEAD_DIM, (t + 1) * HEAD_DIM)
        ext_scr[t, 8:8 + tl, :] = x_t
        conv = convw_ref[CONV_W - 1:CONV_W, lanes] * x_t
        for i in range(CONV_W - 1):
            conv = conv + convw_ref[i:i + 1, lanes] * ext_scr[t, 5 + i:5 + i + tl, :]
        cfin_ref[:, lanes] = ext_scr[t, tl + 5:tl + 8, :]
        ext_scr[t, 0:8, :] = ext_scr[t, tl:tl + 8, :]
        act = _silu(conv)
        if t < DN_HEADS:
            act = _l2norm(act) * (HEAD_DIM ** -0.5)
        elif t < 2 * DN_HEADS:
            act = _l2norm(act)
        qkv_scr[t] = act

    def pool_slab(gi, tok):
        w = POOL_WINDOWS[gi]
        lanes = slice(gi * POOL_GROUP_DIM, (gi + 1) * POOL_GROUP_DIM)
        pext_scr[gi, 16:16 + tl, :] = tok
        s = tok
        for i in range(1, w):
            s = s + pext_scr[gi, 16 - i:16 - i + tl, :]
        pos = l * tl + lax.broadcasted_iota(jnp.int32, (tl, 1), 0)
        cnt = jnp.minimum(w, pos + 1).astype(F32)
        pfin_ref[:, lanes] = pext_scr[gi, tl + 1:tl + 16, :]
        pext_scr[gi, 0:16, :] = pext_scr[gi, tl:tl + 16, :]
        return s / cnt - tok

    group = 2 * HEAD_DIM
    for j in range(QKV_DIM // group):
        y = project(j * group, group)
        conv_slab(2 * j, y[:, :HEAD_DIM])
        conv_slab(2 * j + 1, y[:, HEAD_DIM:])
    beta_t, g_t = _gates(project(G_OFF, GATE_PAD), alog_ref[...], dtb_ref[...])
    beta_scr[...] = beta_t
    g_scr[...] = g_t
    d_groups = []
    for j in range(POOL_DIM // group):
        y = project(P_OFF + j * group, group)
        d_groups.append(pool_slab(2 * j, y[:, :POOL_GROUP_DIM]))
        d_groups.append(pool_slab(2 * j + 1, y[:, POOL_GROUP_DIM:]))
    opool_scr[...] = _pool_project(d_groups, poolw_ref, pscale_ref[...])
    z_scr[...] = project(Z_OFF, DN_DIM)

    c = CHUNK
    row = lax.broadcasted_iota(jnp.int32, (c, c), 0)
    col = lax.broadcasted_iota(jnp.int32, (c, c), 1)
    incl = row >= col
    strict = row > col
    tri = jnp.where(incl, 1.0, 0.0).astype(BF16)

    n_chunks = tl // c
    chains = [(ci, hd) for ci in range(n_chunks) for hd in range(DN_HEADS)]
    gc_alls, gc_ts, beta_alls = [], [], []
    for ci in range(n_chunks):
        rows = slice(ci * c, (ci + 1) * c)
        gc_all = _dot_exact_lhs(tri, g_scr[rows, :])
        gc_alls.append(gc_all)
        gc_ts.append(gc_all.T)
        beta_alls.append(beta_scr[rows, :])
    qs, ks, betas, egs, decs, rhss, k_ends, g_ends = [], [], [], [], [], [], [], []
    for ci, hd in chains:
        rows = slice(ci * c, (ci + 1) * c)
        sl = slice(hd * HEAD_DIM, (hd + 1) * HEAD_DIM)
        lane = DECAY_LANE + hd
        qh, kh, vh = qkv_scr[hd, rows, :], qkv_scr[DN_HEADS + hd, rows, :], qkv_scr[2 * DN_HEADS + hd, rows, :]
        beta = beta_alls[ci][:, hd:hd + 1]
        gc_col = gc_alls[ci][:, lane:lane + 1]
        gc_row = gc_ts[ci][lane:lane + 1, :]
        gc_last = gc_alls[ci][c - 1:c, lane:lane + 1]
        eg = jnp.exp(gc_col)
        decs.append(jnp.where(incl, jnp.exp(jnp.where(incl, gc_col - gc_row, 0.0)), 0.0))
        rhss.append(jnp.concatenate([beta * vh, (beta * eg) * kh], axis=-1))
        k_ends.append(kh * jnp.exp(gc_last - gc_col))
        g_ends.append(jnp.exp(gc_last))
        qs.append(qh); ks.append(kh); betas.append(beta); egs.append(eg)
    kqs = [_dot_nt(jnp.concatenate([kh, qh], axis=0), kh) for kh, qh in zip(ks, qs)]
    ms = [jnp.where(strict, beta * kq[:c] * dec, 0.0) for beta, kq, dec in zip(betas, kqs, decs)]
    qks = [kq[c:] * dec for kq, dec in zip(kqs, decs)]
    t_invs = _inv_unit_lower(ms, row, col, c)
    sols = [_dot(t_inv, rhs) for t_inv, rhs in zip(t_invs, rhss)]
    q_decs = [qh * eg for qh, eg in zip(qs, egs)]

    s_cur = [s_scr[hd] for hd in range(DN_HEADS)]
    o_chunks = []
    for ci in range(n_chunks):
        idx = [ci * DN_HEADS + hd for hd in range(DN_HEADS)]
        wss = [_dot(jnp.concatenate([sols[i][:, HEAD_DIM:], q_decs[i]], axis=0), s) for i, s in zip(idx, s_cur)]
        us = [sols[i][:, :HEAD_DIM] - ws[:c] for i, ws in zip(idx, wss)]
        outs = [ws[c:] + _dot(qks[i], u) for i, ws, u in zip(idx, wss, us)]
        s_cur = [g_ends[i] * s + _dot_tn(k_ends[i], u) for i, s, u in zip(idx, s_cur, us)]
        o_chunks.append(jnp.concatenate(outs, axis=-1))
        if len(o_chunks) == OUT_CHUNKS or ci == n_chunks - 1:
            rows = slice((ci + 1 - len(o_chunks)) * c, (ci + 1) * c)
            o_dn = _gated_out_norm(jnp.concatenate(o_chunks, axis=0), z_scr[rows, :], onorm_ref[...])
            mix = _dot(jnp.concatenate([o_dn, opool_scr[rows, :]], axis=-1), wout_ref[...])
            y_ref[rows, :] = _layer_norm(DN_ALPHA * h_ref[rows, :] + mix, g2_ref[...], b2_ref[...])
            o_chunks = []
    for hd in range(DN_HEADS):
        s_scr[hd] = s_cur[hd]

    @pl.when(l == n_l - 1)
    def _():
        sfin_ref[...] = s_scr[...]


def _mixer_weight_specs(layer):
    return [
        _layer_resident(layer, (D_MODEL, PROJ_DIM)),
        _layer_resident(layer, (CONV_W, QKV_DIM)),
        _layer_resident(layer, (1, GATE_PAD)),
        _layer_resident(layer, (1, GATE_PAD)),
        _layer_resident(layer, (1, HEAD_DIM)),
        _layer_resident(layer, (len(POOL_WINDOWS), POOL_GROUP_DIM, POOL_GROUP_DIM)),
        _layer_resident(layer, (1, POOL_DIM)),
        _layer_resident(layer, (DN_DIM + POOL_DIM, D_MODEL)),
        _layer_resident(layer, (1, D_MODEL)),
        _layer_resident(layer, (1, D_MODEL)),
    ]


def _mixer_prompt(layer, h, *weights):
    bsz, seq, _ = h.shape
    tl = MIX_ROWS
    assert seq % tl == 0 and tl % CHUNK == 0
    out_shape = (
        jax.ShapeDtypeStruct((bsz, seq, D_MODEL), F32),
        jax.ShapeDtypeStruct((bsz, DN_HEADS, HEAD_DIM, HEAD_DIM), F32),
        jax.ShapeDtypeStruct((bsz, CONV_W - 1, QKV_DIM), F32),
        jax.ShapeDtypeStruct((bsz, POOL_BUF, POOL_DIM), F32),
    )
    return pl.pallas_call(
        _mixer_prompt_kernel,
        out_shape=out_shape,
        grid=(bsz, seq // tl),
        in_specs=[
            pl.BlockSpec((None, tl, D_MODEL), lambda b, l: (b, l, 0)),
        ] + _mixer_weight_specs(layer),
        out_specs=(
            pl.BlockSpec((None, tl, D_MODEL), lambda b, l: (b, l, 0)),
            pl.BlockSpec((None, DN_HEADS, HEAD_DIM, HEAD_DIM), lambda b, l: (b, 0, 0, 0)),
            pl.BlockSpec((None, CONV_W - 1, QKV_DIM), lambda b, l: (b, 0, 0)),
            pl.BlockSpec((None, POOL_BUF, POOL_DIM), lambda b, l: (b, 0, 0)),
        ),
        scratch_shapes=[
            pltpu.VMEM((DN_HEADS, HEAD_DIM, HEAD_DIM), F32),
            pltpu.VMEM((QKV_DIM // HEAD_DIM, tl + 8, HEAD_DIM), F32),
            pltpu.VMEM((len(POOL_WINDOWS), tl + 16, POOL_GROUP_DIM), F32),
            pltpu.VMEM((QKV_DIM // HEAD_DIM, tl, HEAD_DIM), F32),
            pltpu.VMEM((tl, GATE_PAD), F32),
            pltpu.VMEM((tl, GATE_PAD), F32),
            pltpu.VMEM((tl, DN_DIM), F32),
            pltpu.VMEM((tl, POOL_DIM), F32),
        ],
        compiler_params=pltpu.CompilerParams(dimension_semantics=("arbitrary", "arbitrary")),
        name="mixer_prompt",
    )(h, *weights)


def _mixer_sample_kernel(h_ref, s_ref, cbuf_ref, pbuf_ref, win_ref, convw_ref, alog_ref, dtb_ref, onorm_ref,
                         poolw_ref, pscale_ref, wout_ref, g2_ref, b2_ref, sprev_ref,
                         y_ref, snew_ref, cnew_ref, pnew_ref,
                         q_scr, k_scr, v_scr, beta_scr, eg_scr, z_scr, opool_scr, o_scr):
    del sprev_ref
    i = pl.program_id(0)
    n_i = pl.num_programs(0)

    @pl.when(i == 0)
    def _():
        proj = jnp.dot(h_ref[...].astype(BF16), win_ref[...], preferred_element_type=F32)
        qkv = proj[:, 0:QKV_DIM]
        conv = convw_ref[CONV_W - 1:CONV_W, :] * qkv
        for j in range(CONV_W - 1):
            conv = conv + convw_ref[j:j + 1, :] * cbuf_ref[:, j, :]
        for j in range(CONV_W - 2):
            cnew_ref[:, j, :] = cbuf_ref[:, j + 1, :]
        cnew_ref[:, CONV_W - 2, :] = qkv
        q, k, v = _qkv_heads(conv)
        q_scr[...] = q
        k_scr[...] = k
        v_scr[...] = v
        beta_t, g_t = _gates(proj[:, G_OFF:G_OFF + GATE_PAD], alog_ref[...], dtb_ref[...])
        beta_scr[...] = beta_t
        eg_scr[...] = jnp.exp(g_t)
        z_scr[...] = proj[:, Z_OFF:Z_OFF + DN_DIM]

        p = proj[:, P_OFF:P_OFF + POOL_DIM]
        d_groups = []
        for gi, w in enumerate(POOL_WINDOWS):
            lo = gi * POOL_GROUP_DIM
            tok = p[:, lo:lo + POOL_GROUP_DIM]
            s = tok
            for back in range(1, w):
                s = s + pbuf_ref[:, POOL_BUF - back, lo:lo + POOL_GROUP_DIM]
            cnt = float(min(w, PAST_LEN + 1))
            d_groups.append(s / cnt - tok)
        opool_scr[...] = _pool_project(d_groups, poolw_ref, pscale_ref[...])
        for j in range(POOL_BUF - 1):
            pnew_ref[:, j, :] = pbuf_ref[:, j + 1, :]
        pnew_ref[:, POOL_BUF - 1, :] = p

    eye = jnp.where(lax.broadcasted_iota(jnp.int32, (HEAD_DIM, HEAD_DIM), 0)
                    == lax.broadcasted_iota(jnp.int32, (HEAD_DIM, HEAD_DIM), 1), 1.0, 0.0).astype(F32)

    rows = pl.ds(pl.multiple_of(i * SAMPLE_TILE, SAMPLE_TILE), SAMPLE_TILE)
    q_t, k_t, v_t = q_scr[rows, :], k_scr[rows, :], v_scr[rows, :]
    beta_t, a_t = beta_scr[rows, :], eg_scr[rows, :]
    o_rows = []
    for j in range(SAMPLE_TILE):
        o_heads = []
        for hd in range(DN_HEADS):
            sl = slice(hd * HEAD_DIM, (hd + 1) * HEAD_DIM)
            qh = q_t[j:j + 1, sl]
            kh = k_t[j:j + 1, sl]
            vh = v_t[j:j + 1, sl]
            beta = beta_t[j:j + 1, hd:hd + 1]
            a = a_t[j:j + 1, DECAY_LANE + hd:DECAY_LANE + hd + 1]
            k_col = jnp.sum(eye * kh, axis=-1, keepdims=True)
            q_col = jnp.sum(eye * qh, axis=-1, keepdims=True)
            s_old = s_ref[j, hd]
            ks = jnp.sum(k_col * s_old, axis=0, keepdims=True)
            qs = jnp.sum(q_col * s_old, axis=0, keepdims=True)
            u = beta * vh - (beta * a) * ks
            qk = jnp.sum(qh * kh, axis=-1, keepdims=True)
            o_heads.append(a * qs + qk * u)
            snew_ref[j, hd] = a * s_old + k_col * u
        o_rows.append(jnp.concatenate(o_heads, axis=-1))
    o_scr[rows, :] = jnp.concatenate(o_rows, axis=0)

    @pl.when(i == n_i - 1)
    def _():
        o_dn = _gated_out_norm(o_scr[...], z_scr[...], onorm_ref[...])
        mix = _dot(jnp.concatenate([o_dn, opool_scr[...]], axis=-1), wout_ref[...])
        y_ref[...] = _layer_norm(DN_ALPHA * h_ref[...] + mix, g2_ref[...], b2_ref[...])


def _mixer_sample(layer, h, s_all, cbuf_all, pbuf_all, s_out_prev, *weights):
    bsz = h.shape[0]
    assert bsz % SAMPLE_TILE == 0
    cshape = (bsz, CONV_W - 1, QKV_DIM)
    pshape = (bsz, POOL_BUF, POOL_DIM)
    out_shape = (
        jax.ShapeDtypeStruct((bsz, D_MODEL), F32),
        jax.ShapeDtypeStruct(s_all.shape, F32),
        jax.ShapeDtypeStruct(cshape, F32),
        jax.ShapeDtypeStruct(pshape, F32),
    )
    state_spec = pl.BlockSpec((None, SAMPLE_TILE, DN_HEADS, HEAD_DIM, HEAD_DIM), lambda i: (layer, i, 0, 0, 0))
    n_in = 4 + 10
    return pl.pallas_call(
        _mixer_sample_kernel,
        out_shape=out_shape,
        grid=(bsz // SAMPLE_TILE,),
        in_specs=[
            _resident((bsz, D_MODEL)),
            state_spec,
            _layer_resident(layer, cshape),
            _layer_resident(layer, pshape),
        ] + _mixer_weight_specs(layer) + [pl.BlockSpec(memory_space=pl.ANY)],
        out_specs=(
            pl.BlockSpec((bsz, D_MODEL), lambda i: (0, 0)),
            state_spec,
            pl.BlockSpec(cshape, lambda i: (0, 0, 0)),
            pl.BlockSpec(pshape, lambda i: (0, 0, 0)),
        ),
        input_output_aliases={n_in: 1},
        scratch_shapes=[
            pltpu.VMEM((bsz, DN_DIM), F32),
            pltpu.VMEM((bsz, DN_DIM), F32),
            pltpu.VMEM((bsz, DN_DIM), F32),
            pltpu.VMEM((bsz, GATE_PAD), F32),
            pltpu.VMEM((bsz, GATE_PAD), F32),
            pltpu.VMEM((bsz, DN_DIM), F32),
            pltpu.VMEM((bsz, POOL_DIM), F32),
            pltpu.VMEM((bsz, DN_DIM), F32),
        ],
        compiler_params=pltpu.CompilerParams(dimension_semantics=("arbitrary",)),
        name="mixer_sample",
    )(h, s_all, cbuf_all, pbuf_all, *weights, s_out_prev)


def _pad_gate_param(v):
    out = jnp.zeros((v.shape[0], 1, GATE_PAD), F32)
    return out.at[:, 0, DECAY_LANE:DECAY_LANE + DN_HEADS].set(v.astype(F32))


def kernel(x_prompt, x_sample, state_delta, state_conv, state_pool, ln1_g, ln1_b, ffn1_w_gate, ffn1_w_up, ffn1_w_down, w_in, conv_w, a_log, dt_bias, onorm_g, pool_w, pool_scale, w_out, ln2_g, ln2_b, ffn2_w_gate, ffn2_w_up, ffn2_w_down, ln3_g, ln3_b):
    bsz, seq, _ = x_prompt.shape
    dec_b = x_sample.shape[0]

    b_off = QKV_DIM + DN_DIM
    p_off = b_off + 2 * DN_HEADS
    w_in16 = w_in.astype(BF16)
    win = jnp.concatenate(
        [w_in16[:, :, :b_off], w_in16[:, :, p_off:], w_in16[:, :, b_off:p_off],
         jnp.zeros((DEPTH, D_MODEL, GATE_PAD - 2 * DN_HEADS), BF16)], axis=-1)
    wout = w_out.astype(BF16)
    poolw = pool_w.astype(BF16)
    f1g, f1u, f1d = ffn1_w_gate.astype(BF16), ffn1_w_up.astype(BF16), ffn1_w_down.astype(BF16)
    f2g, f2u, f2d = ffn2_w_gate.astype(BF16), ffn2_w_up.astype(BF16), ffn2_w_down.astype(BF16)
    alog = _pad_gate_param(a_log)
    dtb = _pad_gate_param(dt_bias)
    row = lambda v: v.reshape(DEPTH, 1, -1)
    ln1g, ln1b, ln2g, ln2b, ln3g, ln3b = map(row, (ln1_g, ln1_b, ln2_g, ln2_b, ln3_g, ln3_b))
    onorm = row(onorm_g)
    pscale = row(pool_scale)

    mix_w = (win, conv_w, alog, dtb, onorm, poolw, pscale, wout, ln2g, ln2b)
    xp = x_prompt.reshape(bsz * seq, D_MODEL)
    xs = x_sample.reshape(dec_b, D_MODEL)
    dp, cp, pp, cs, ps = [], [], [], [], []
    delta_sample = jnp.zeros(state_delta.shape, F32)
    for l in range(DEPTH):
        hp, hs = _ffn_ln(l, xp, xs, f1g, f1u, f1d, ln1g, ln1b)
        hp, s_p, c_p, p_p = _mixer_prompt(l, hp.reshape(bsz, seq, D_MODEL), *mix_w)
        hs, delta_sample, c_s, p_s = _mixer_sample(l, hs, state_delta, state_conv, state_pool, delta_sample, *mix_w)
        xp, xs = _ffn_ln(l, hp.reshape(bsz * seq, D_MODEL), hs, f2g, f2u, f2d, ln3g, ln3b)
        dp.append(s_p); cp.append(c_p); pp.append(p_p)
        cs.append(c_s); ps.append(p_s)
    return (xp.reshape(bsz, seq, D_MODEL), xs.reshape(dec_b, 1, D_MODEL),
            jnp.stack(dp), jnp.stack(cp), jnp.stack(pp), delta_sample, jnp.stack(cs), jnp.stack(ps))
```

```python
import functools

import jax
import jax.numpy as jnp
from jax import lax
from jax.experimental import pallas as pl
from jax.experimental.pallas import tpu as pltpu

F32 = jnp.float32
BF16 = jnp.bfloat16

D_MODEL = 1024
DEPTH = 4
DN_HEADS = 4
HEAD_DIM = 128
DN_DIM = DN_HEADS * HEAD_DIM
POOL_DIM = 512
QKV_DIM = 3 * DN_DIM
CONV_W = 4
POOL_WINDOWS = (2, 4, 8, 16)
POOL_GROUP_DIM = POOL_DIM // len(POOL_WINDOWS)
POOL_BUF = max(POOL_WINDOWS) - 1
D_FF = 2816
PAST_LEN = 16384
DN_ALPHA = (2.0 * DEPTH) ** 0.25
LN_EPS = 1e-5
RMS_EPS = 1e-6
L2_EPS = 1e-6

N_GATE = 2 * DN_HEADS
Z_OFF = QKV_DIM
GATE_OFF = QKV_DIM + DN_DIM
POOL_OFF = GATE_OFF + N_GATE
IN_DIM = POOL_OFF + POOL_DIM
GATE_PAD = 128
DECAY_LANE = DN_HEADS

CHUNK = 128
INV_BASE = 8
FFN_ROWS = 1024
FFN_SUB_ROWS = 256
MIX_ROWS = 512
OUT_CHUNKS = 2
CAST_STEPS = 2
BF16_SUBLANES = 16
SAMPLE_TILE = 8


def _dot(a, b):
    return jnp.dot(a.astype(BF16), b.astype(BF16), preferred_element_type=F32)


def _dot_nt(a, b):
    return lax.dot_general(a.astype(BF16), b.astype(BF16), (((1,), (1,)), ((), ())),
                           preferred_element_type=F32)


def _dot_tn(a, b):
    return lax.dot_general(a.astype(BF16), b.astype(BF16), (((0,), (0,)), ((), ())),
                           preferred_element_type=F32)


def _dot_exact_lhs(a_bf16, b):
    b1 = b.astype(BF16)
    r1 = b - b1.astype(F32)
    b2 = r1.astype(BF16)
    b3 = (r1 - b2.astype(F32)).astype(BF16)
    d = functools.partial(jnp.dot, preferred_element_type=F32)
    return d(a_bf16, b1) + (d(a_bf16, b2) + d(a_bf16, b3))


def _silu(x):
    return x * jax.nn.sigmoid(x)


def _softplus(x):
    return jnp.maximum(x, 0.0) + jnp.log1p(jnp.exp(-jnp.abs(x)))


def _layer_norm(y, g, b):
    mu = jnp.mean(y, axis=-1, keepdims=True)
    yc = y - mu
    var = jnp.mean(yc * yc, axis=-1, keepdims=True)
    return yc * lax.rsqrt(var + LN_EPS) * g + b


def _l2norm(x):
    return x * lax.rsqrt(jnp.sum(x * x, axis=-1, keepdims=True) + L2_EPS)


def _ffn_ln_rows(x, wg_ref, wu_ref, wd_ref, g_ref, b_ref):
    xb = x.astype(BF16)
    gate = jnp.dot(xb, wg_ref[...], preferred_element_type=F32)
    up = jnp.dot(xb, wu_ref[...], preferred_element_type=F32)
    act = (_silu(gate) * up).astype(BF16)
    ff = jnp.dot(act, wd_ref[...], preferred_element_type=F32)
    return _layer_norm(DN_ALPHA * x + 0.5 * ff, g_ref[...], b_ref[...])


def _ffn_ln_kernel(x_ref, xs_ref, wg_ref, wu_ref, wd_ref, g_ref, b_ref, o_ref, os_ref):
    weights = (wg_ref, wu_ref, wd_ref, g_ref, b_ref)
    tm = x_ref.shape[0]
    for r in range(tm // FFN_SUB_ROWS):
        rows = slice(r * FFN_SUB_ROWS, (r + 1) * FFN_SUB_ROWS)
        o_ref[rows, :] = _ffn_ln_rows(x_ref[rows, :], *weights)

    @pl.when(pl.program_id(0) == pl.num_programs(0) - 1)
    def _():
        os_ref[...] = _ffn_ln_rows(xs_ref[...], *weights)


def _resident(shape):
    zeros = (0,) * len(shape)
    return pl.BlockSpec(shape, lambda *_: zeros, pipeline_mode=pl.Buffered(1))


def _layer_resident(layer, shape):
    index = (layer,) + (0,) * len(shape)
    return pl.BlockSpec((None,) + tuple(shape), lambda *_: index, pipeline_mode=pl.Buffered(1))


def _ffn_ln(layer, x, xs, wg, wu, wd, g, b):
    rows = x.shape[0]
    tm = FFN_ROWS
    assert rows % tm == 0 and tm % FFN_SUB_ROWS == 0
    return pl.pallas_call(
        _ffn_ln_kernel,
        out_shape=(jax.ShapeDtypeStruct((rows, D_MODEL), F32), jax.ShapeDtypeStruct(xs.shape, F32)),
        grid=(rows // tm,),
        in_specs=[
            pl.BlockSpec((tm, D_MODEL), lambda i: (i, 0)),
            _resident(xs.shape),
            _resident((D_MODEL, D_FF)),
            _resident((D_MODEL, D_FF)),
            _resident((D_FF, D_MODEL)),
            _layer_resident(layer, (1, D_MODEL)),
            _layer_resident(layer, (1, D_MODEL)),
        ],
        out_specs=(pl.BlockSpec((tm, D_MODEL), lambda i: (i, 0)),
                   pl.BlockSpec(xs.shape, lambda i: (0, 0))),
        compiler_params=pltpu.CompilerParams(dimension_semantics=("arbitrary",)),
        name="ffn_ln",
    )(x, xs, wg, wu, wd, g, b)


def _gates(ba, alog, dtb):
    beta = jax.nn.sigmoid(ba)
    g = -jnp.exp(alog) * _softplus(ba + dtb)
    return beta, g


def _qkv_heads(conv_out):
    act = _silu(conv_out)
    qs, ks = [], []
    for h in range(DN_HEADS):
        lo = h * HEAD_DIM
        qs.append(_l2norm(act[:, lo:lo + HEAD_DIM]) * (HEAD_DIM ** -0.5))
        ks.append(_l2norm(act[:, DN_DIM + lo:DN_DIM + lo + HEAD_DIM]))
    return jnp.concatenate(qs, -1), jnp.concatenate(ks, -1), act[:, 2 * DN_DIM:]


def _gated_out_norm(o, z, onorm_g):
    outs = []
    for h in range(DN_HEADS):
        sl = slice(h * HEAD_DIM, (h + 1) * HEAD_DIM)
        oh = o[:, sl]
        oh = oh * lax.rsqrt(jnp.mean(oh * oh, axis=-1, keepdims=True) + RMS_EPS)
        outs.append(oh * onorm_g * _silu(z[:, sl]))
    return jnp.concatenate(outs, -1)


def _pool_project(d_groups, poolw_ref, pscale):
    ys = [_dot(d, poolw_ref[gi]) for gi, d in enumerate(d_groups)]
    return jnp.concatenate(ys, -1) * pscale


def _inv_unit_lower(ms, row, col, size):
    eye = jnp.where(row == col, 1.0, 0.0).astype(F32)
    base_bits = INV_BASE.bit_length() - 1
    base_mask = (row >> base_bits) == (col >> base_bits)
    ps = [jnp.where(base_mask, m, 0.0) for m in ms]
    xs = [eye - p for p in ps]
    ps = [_dot(p, p) for p in ps]
    for step in range(base_bits - 1):
        if step < base_bits - 2:
            xps = [_dot(jnp.concatenate([x, p], axis=0), p) for x, p in zip(xs, ps)]
            xs = [x + xp[:size] for x, xp in zip(xs, xps)]
            ps = [xp[size:] for xp in xps]
        else:
            xs = [x + _dot(x, p) for x, p in zip(xs, ps)]
    bits = base_bits
    while (1 << bits) < size:
        same_pair = (row >> (bits + 1)) == (col >> (bits + 1))
        lower_left = (((row >> bits) & 1) == 1) & (((col >> bits) & 1) == 0)
        mask = same_pair & lower_left
        ys = [_dot(jnp.where(mask, m, 0.0), x) for m, x in zip(ms, xs)]
        xs = [x - _dot(x, y) for x, y in zip(xs, ys)]
        bits += 1
    return xs


def _mixer_prompt_kernel(n_cast, h_ref, win_ref, convw_ref, alog_ref, dtb_ref, onorm_ref, poolw_ref, pscale_ref,
                         wout_ref, g2_ref, b2_ref, *refs):
    cast_src = refs[:n_cast]
    y_ref, sfin_ref, cfin_ref, pfin_ref = refs[n_cast:n_cast + 4]
    cast_dst = refs[n_cast + 4:2 * n_cast + 4]
    s_scr, ext_scr, pext_scr, qkv_scr, beta_scr, g_scr, z_scr, opool_scr, pw_scr, gw_scr = refs[2 * n_cast + 4:]
    tl = h_ref.shape[0]
    l = pl.program_id(1)
    n_l = pl.num_programs(1)

    @pl.when((pl.program_id(0) * n_l + l) % CAST_STEPS == 0)
    def _():
        for src, dst in zip(cast_src, cast_dst):
            dst[...] = src[...].astype(BF16)

    @pl.when(l == 0)
    def _():
        s_scr[...] = jnp.zeros_like(s_scr)
        ext_scr[:, 0:8, :] = jnp.zeros((QKV_DIM // HEAD_DIM, 8, HEAD_DIM), F32)
        pext_scr[:, 0:16, :] = jnp.zeros((len(POOL_WINDOWS), 16, POOL_GROUP_DIM), F32)

    hb = h_ref[...].astype(BF16)

    @pl.when((pl.program_id(0) == 0) & (l == 0))
    def _():
        pw_scr[...] = win_ref[:, POOL_OFF:POOL_OFF + POOL_DIM]
        gw_scr[...] = jnp.concatenate([win_ref[:, GATE_OFF:GATE_OFF + N_GATE],
                                       jnp.zeros((D_MODEL, GATE_PAD - N_GATE), BF16)], axis=-1)

    def project(w_ref, lo, width):
        return jnp.dot(hb, w_ref[:, lo:lo + width], preferred_element_type=F32)

    def conv_slab(t, x_t):
        lanes = slice(t * HEAD_DIM, (t + 1) * HEAD_DIM)
        ext_scr[t, 8:8 + tl, :] = x_t
        conv = convw_ref[CONV_W - 1:CONV_W, lanes] * x_t
        for i in range(CONV_W - 1):
            conv = conv + convw_ref[i:i + 1, lanes] * ext_scr[t, 5 + i:5 + i + tl, :]
        cfin_ref[:, lanes] = ext_scr[t, tl + 5:tl + 8, :]
        ext_scr[t, 0:8, :] = ext_scr[t, tl:tl + 8, :]
        act = _silu(conv)
        if t < DN_HEADS:
            act = _l2norm(act) * (HEAD_DIM ** -0.5)
        elif t < 2 * DN_HEADS:
            act = _l2norm(act)
        qkv_scr[t] = act

    def pool_slab(gi, tok):
        w = POOL_WINDOWS[gi]
        lanes = slice(gi * POOL_GROUP_DIM, (gi + 1) * POOL_GROUP_DIM)
        pext_scr[gi, 16:16 + tl, :] = tok
        s = tok
        for i in range(1, w):
            s = s + pext_scr[gi, 16 - i:16 - i + tl, :]
        pos = l * tl + lax.broadcasted_iota(jnp.int32, (tl, 1), 0)
        cnt = jnp.minimum(w, pos + 1).astype(F32)
        pfin_ref[:, lanes] = pext_scr[gi, tl + 1:tl + 16, :]
        pext_scr[gi, 0:16, :] = pext_scr[gi, tl:tl + 16, :]
        return s / cnt - tok

    group = 2 * HEAD_DIM
    for j in range(QKV_DIM // group):
        y = project(win_ref, j * group, group)
        conv_slab(2 * j, y[:, :HEAD_DIM])
        conv_slab(2 * j + 1, y[:, HEAD_DIM:])
    beta_t, g_t = _gates(project(gw_scr, 0, GATE_PAD), alog_ref[...], dtb_ref[...])
    beta_scr[...] = beta_t
    g_scr[...] = g_t
    d_groups = []
    for j in range(POOL_DIM // group):
        y = project(pw_scr, j * group, group)
        d_groups.append(pool_slab(2 * j, y[:, :POOL_GROUP_DIM]))
        d_groups.append(pool_slab(2 * j + 1, y[:, POOL_GROUP_DIM:]))
    opool_scr[...] = _pool_project(d_groups, poolw_ref, pscale_ref[...])
    z_scr[...] = project(win_ref, Z_OFF, DN_DIM)

    c = CHUNK
    row = lax.broadcasted_iota(jnp.int32, (c, c), 0)
    col = lax.broadcasted_iota(jnp.int32, (c, c), 1)
    incl = row >= col
    strict = row > col
    tri = jnp.where(incl, 1.0, 0.0).astype(BF16)

    n_chunks = tl // c
    chains = [(ci, hd) for ci in range(n_chunks) for hd in range(DN_HEADS)]
    gc_alls, gc_ts, beta_alls = [], [], []
    for ci in range(n_chunks):
        rows = slice(ci * c, (ci + 1) * c)
        gc_all = _dot_exact_lhs(tri, g_scr[rows, :])
        gc_alls.append(gc_all)
        gc_ts.append(gc_all.T)
        beta_alls.append(beta_scr[rows, :])
    qs, ks, betas, egs, decs, rhss, k_ends, g_ends = [], [], [], [], [], [], [], []
    for ci, hd in chains:
        rows = slice(ci * c, (ci + 1) * c)
        sl = slice(hd * HEAD_DIM, (hd + 1) * HEAD_DIM)
        lane = DECAY_LANE + hd
        qh, kh, vh = qkv_scr[hd, rows, :], qkv_scr[DN_HEADS + hd, rows, :], qkv_scr[2 * DN_HEADS + hd, rows, :]
        beta = beta_alls[ci][:, hd:hd + 1]
        gc_col = gc_alls[ci][:, lane:lane + 1]
        gc_row = gc_ts[ci][lane:lane + 1, :]
        gc_last = gc_alls[ci][c - 1:c, lane:lane + 1]
        eg = jnp.exp(gc_col)
        decs.append(jnp.where(incl, jnp.exp(jnp.where(incl, gc_col - gc_row, 0.0)), 0.0))
        rhss.append(jnp.concatenate([beta * vh, (beta * eg) * kh], axis=-1))
        k_ends.append(kh * jnp.exp(gc_last - gc_col))
        g_ends.append(jnp.exp(gc_last))
        qs.append(qh); ks.append(kh); betas.append(beta); egs.append(eg)
    kqs = [_dot_nt(jnp.concatenate([kh, qh], axis=0), kh) for kh, qh in zip(ks, qs)]
    ms = [jnp.where(strict, beta * kq[:c] * dec, 0.0) for beta, kq, dec in zip(betas, kqs, decs)]
    qks = [kq[c:] * dec for kq, dec in zip(kqs, decs)]
    t_invs = _inv_unit_lower(ms, row, col, c)
    sols = [_dot(t_inv, rhs) for t_inv, rhs in zip(t_invs, rhss)]
    q_decs = [qh * eg for qh, eg in zip(qs, egs)]

    s_cur = [s_scr[hd] for hd in range(DN_HEADS)]
    o_chunks = []
    for ci in range(n_chunks):
        idx = [ci * DN_HEADS + hd for hd in range(DN_HEADS)]
        wss = [_dot(jnp.concatenate([sols[i][:, HEAD_DIM:], q_decs[i]], axis=0), s) for i, s in zip(idx, s_cur)]
        us = [sols[i][:, :HEAD_DIM] - ws[:c] for i, ws in zip(idx, wss)]
        outs = [ws[c:] + _dot(qks[i], u) for i, ws, u in zip(idx, wss, us)]
        s_cur = [g_ends[i] * s + _dot_tn(k_ends[i], u) for i, s, u in zip(idx, s_cur, us)]
        o_chunks.append(jnp.concatenate(outs, axis=-1))
        if len(o_chunks) == OUT_CHUNKS or ci == n_chunks - 1:
            rows = slice((ci + 1 - len(o_chunks)) * c, (ci + 1) * c)
            o_dn = _gated_out_norm(jnp.concatenate(o_chunks, axis=0), z_scr[rows, :], onorm_ref[...])
            mix = _dot(jnp.concatenate([o_dn, opool_scr[rows, :]], axis=-1), wout_ref[...])
            y_ref[rows, :] = _layer_norm(DN_ALPHA * h_ref[rows, :] + mix, g2_ref[...], b2_ref[...])
            o_chunks = []
    for hd in range(DN_HEADS):
        s_scr[hd] = s_cur[hd]

    @pl.when(l == n_l - 1)
    def _():
        sfin_ref[...] = s_scr[...]


def _mixer_weight_specs(layer):
    return [
        _layer_resident(layer, (D_MODEL, IN_DIM)),
        _layer_resident(layer, (CONV_W, QKV_DIM)),
        _layer_resident(layer, (1, GATE_PAD)),
        _layer_resident(layer, (1, GATE_PAD)),
        _layer_resident(layer, (1, HEAD_DIM)),
        _layer_resident(layer, (len(POOL_WINDOWS), POOL_GROUP_DIM, POOL_GROUP_DIM)),
        _layer_resident(layer, (1, POOL_DIM)),
        _layer_resident(layer, (DN_DIM + POOL_DIM, D_MODEL)),
        _layer_resident(layer, (1, D_MODEL)),
        _layer_resident(layer, (1, D_MODEL)),
    ]


def _mixer_prompt(layer, h, weights, cast_jobs):
    bsz, seq, _ = h.shape
    tl = MIX_ROWS
    n_l = seq // tl
    assert seq % tl == 0 and tl % CHUNK == 0 and (bsz * n_l) % CAST_STEPS == 0
    n_blocks = bsz * n_l // CAST_STEPS
    out_shape = [
        jax.ShapeDtypeStruct((bsz, seq, D_MODEL), F32),
        jax.ShapeDtypeStruct((bsz, DN_HEADS, HEAD_DIM, HEAD_DIM), F32),
        jax.ShapeDtypeStruct((bsz, CONV_W - 1, QKV_DIM), F32),
        jax.ShapeDtypeStruct((bsz, POOL_BUF, POOL_DIM), F32),
    ]
    out_specs = [
        pl.BlockSpec((None, tl, D_MODEL), lambda b, l: (b, l, 0)),
        pl.BlockSpec((None, DN_HEADS, HEAD_DIM, HEAD_DIM), lambda b, l: (b, 0, 0, 0)),
        pl.BlockSpec((None, CONV_W - 1, QKV_DIM), lambda b, l: (b, 0, 0)),
        pl.BlockSpec((None, POOL_BUF, POOL_DIM), lambda b, l: (b, 0, 0)),
    ]
    cast_specs = []
    for w_all, w_layer in cast_jobs:
        _, n_rows, n_cols = w_all.shape
        assert n_rows % (n_blocks * BF16_SUBLANES) == 0
        blk = n_rows // n_blocks
        cast_specs.append(pl.BlockSpec((None, blk, n_cols),
                                       lambda b, l, w_layer=w_layer: (w_layer, (b * n_l + l) // CAST_STEPS, 0)))
        out_shape.append(jax.ShapeDtypeStruct((n_rows, n_cols), BF16))
        out_specs.append(pl.BlockSpec((blk, n_cols), lambda b, l: ((b * n_l + l) // CAST_STEPS, 0)))
    return pl.pallas_call(
        functools.partial(_mixer_prompt_kernel, len(cast_jobs)),
        out_shape=tuple(out_shape),
        grid=(bsz, n_l),
        in_specs=[
            pl.BlockSpec((None, tl, D_MODEL), lambda b, l: (b, l, 0)),
        ] + _mixer_weight_specs(layer) + cast_specs,
        out_specs=tuple(out_specs),
        scratch_shapes=[
            pltpu.VMEM((DN_HEADS, HEAD_DIM, HEAD_DIM), F32),
            pltpu.VMEM((QKV_DIM // HEAD_DIM, tl + 8, HEAD_DIM), F32),
            pltpu.VMEM((len(POOL_WINDOWS), tl + 16, POOL_GROUP_DIM), F32),
            pltpu.VMEM((QKV_DIM // HEAD_DIM, tl, HEAD_DIM), F32),
            pltpu.VMEM((tl, GATE_PAD), F32),
            pltpu.VMEM((tl, GATE_PAD), F32),
            pltpu.VMEM((tl, DN_DIM), F32),
            pltpu.VMEM((tl, POOL_DIM), F32),
            pltpu.VMEM((D_MODEL, POOL_DIM), BF16),
            pltpu.VMEM((D_MODEL, GATE_PAD), BF16),
        ],
        compiler_params=pltpu.CompilerParams(dimension_semantics=("arbitrary", "arbitrary")),
        name="mixer_prompt",
    )(h, *weights, *[w_all for w_all, _ in cast_jobs])


def _mixer_sample_kernel(h_ref, s_ref, cbuf_ref, pbuf_ref, win_ref, convw_ref, alog_ref, dtb_ref, onorm_ref,
                         poolw_ref, pscale_ref, wout_ref, g2_ref, b2_ref, sprev_ref,
                         y_ref, snew_ref, cnew_ref, pnew_ref,
                         q_scr, k_scr, v_scr, beta_scr, eg_scr, z_scr, opool_scr, o_scr):
    del sprev_ref
    i = pl.program_id(0)
    n_i = pl.num_programs(0)

    @pl.when(i == 0)
    def _():
        proj = jnp.dot(h_ref[...].astype(BF16), win_ref[...], preferred_element_type=F32)
        qkv = proj[:, 0:QKV_DIM]
        conv = convw_ref[CONV_W - 1:CONV_W, :] * qkv
        for j in range(CONV_W - 1):
            conv = conv + convw_ref[j:j + 1, :] * cbuf_ref[:, j, :]
        for j in range(CONV_W - 2):
            cnew_ref[:, j, :] = cbuf_ref[:, j + 1, :]
        cnew_ref[:, CONV_W - 2, :] = qkv
        q, k, v = _qkv_heads(conv)
        q_scr[...] = q
        k_scr[...] = k
        v_scr[...] = v
        ba = jnp.concatenate([proj[:, GATE_OFF:GATE_OFF + N_GATE],
                              jnp.zeros((proj.shape[0], GATE_PAD - N_GATE), F32)], axis=-1)
        beta_t, g_t = _gates(ba, alog_ref[...], dtb_ref[...])
        beta_scr[...] = beta_t
        eg_scr[...] = jnp.exp(g_t)
        z_scr[...] = proj[:, Z_OFF:Z_OFF + DN_DIM]

        p = proj[:, POOL_OFF:POOL_OFF + POOL_DIM]
        d_groups = []
        for gi, w in enumerate(POOL_WINDOWS):
            lo = gi * POOL_GROUP_DIM
            tok = p[:, lo:lo + POOL_GROUP_DIM]
            s = tok
            for back in range(1, w):
                s = s + pbuf_ref[:, POOL_BUF - back, lo:lo + POOL_GROUP_DIM]
            cnt = float(min(w, PAST_LEN + 1))
            d_groups.append(s / cnt - tok)
        opool_scr[...] = _pool_project(d_groups, poolw_ref, pscale_ref[...])
        for j in range(POOL_BUF - 1):
            pnew_ref[:, j, :] = pbuf_ref[:, j + 1, :]
        pnew_ref[:, POOL_BUF - 1, :] = p

    rows = pl.ds(pl.multiple_of(i * SAMPLE_TILE, SAMPLE_TILE), SAMPLE_TILE)
    q_t, k_t, v_t = q_scr[rows, :], k_scr[rows, :], v_scr[rows, :]
    beta_t, a_t = beta_scr[rows, :], eg_scr[rows, :]
    eye = jnp.where(lax.broadcasted_iota(jnp.int32, (HEAD_DIM, HEAD_DIM), 0)
                    == lax.broadcasted_iota(jnp.int32, (HEAD_DIM, HEAD_DIM), 1), 1.0, 0.0).astype(F32)
    o_heads = [[] for _ in range(SAMPLE_TILE)]
    for hd in range(DN_HEADS):
        sl = slice(hd * HEAD_DIM, (hd + 1) * HEAD_DIM)
        for j in range(SAMPLE_TILE):
            qh = q_t[j:j + 1, sl]
            kh = k_t[j:j + 1, sl]
            vh = v_t[j:j + 1, sl]
            beta = beta_t[j:j + 1, hd:hd + 1]
            a = a_t[j:j + 1, DECAY_LANE + hd:DECAY_LANE + hd + 1]
            k_col = jnp.sum(eye * kh, axis=-1, keepdims=True)
            q_col = jnp.sum(eye * qh, axis=-1, keepdims=True)
            s_old = s_ref[j, hd]
            ks = jnp.sum(k_col * s_old, axis=0, keepdims=True)
            qs = jnp.sum(q_col * s_old, axis=0, keepdims=True)
            u = beta * vh - (beta * a) * ks
            qk = jnp.sum(qh * kh, axis=-1, keepdims=True)
            o_heads[j].append(a * qs + qk * u)
            snew_ref[j, hd] = a * s_old + k_col * u
    o_scr[rows, :] = jnp.concatenate([jnp.concatenate(o_h, axis=-1) for o_h in o_heads], axis=0)

    @pl.when(i == n_i - 1)
    def _():
        o_dn = _gated_out_norm(o_scr[...], z_scr[...], onorm_ref[...])
        mix = _dot(jnp.concatenate([o_dn, opool_scr[...]], axis=-1), wout_ref[...])
        y_ref[...] = _layer_norm(DN_ALPHA * h_ref[...] + mix, g2_ref[...], b2_ref[...])


def _mixer_sample(layer, h, s_all, cbuf_all, pbuf_all, s_out_prev, *weights):
    bsz = h.shape[0]
    assert bsz % SAMPLE_TILE == 0
    cshape = (bsz, CONV_W - 1, QKV_DIM)
    pshape = (bsz, POOL_BUF, POOL_DIM)
    out_shape = (
        jax.ShapeDtypeStruct((bsz, D_MODEL), F32),
        jax.ShapeDtypeStruct(s_all.shape, F32),
        jax.ShapeDtypeStruct(cshape, F32),
        jax.ShapeDtypeStruct(pshape, F32),
    )
    state_spec = pl.BlockSpec((None, SAMPLE_TILE, DN_HEADS, HEAD_DIM, HEAD_DIM), lambda i: (layer, i, 0, 0, 0))
    n_in = 4 + 10
    return pl.pallas_call(
        _mixer_sample_kernel,
        out_shape=out_shape,
        grid=(bsz // SAMPLE_TILE,),
        in_specs=[
            _resident((bsz, D_MODEL)),
            state_spec,
            _layer_resident(layer, cshape),
            _layer_resident(layer, pshape),
        ] + _mixer_weight_specs(layer) + [pl.BlockSpec(memory_space=pl.ANY)],
        out_specs=(
            pl.BlockSpec((bsz, D_MODEL), lambda i: (0, 0)),
            state_spec,
            pl.BlockSpec(cshape, lambda i: (0, 0, 0)),
            pl.BlockSpec(pshape, lambda i: (0, 0, 0)),
        ),
        input_output_aliases={n_in: 1},
        scratch_shapes=[
            pltpu.VMEM((bsz, DN_DIM), F32),
            pltpu.VMEM((bsz, DN_DIM), F32),
            pltpu.VMEM((bsz, DN_DIM), F32),
            pltpu.VMEM((bsz, GATE_PAD), F32),
            pltpu.VMEM((bsz, GATE_PAD), F32),
            pltpu.VMEM((bsz, DN_DIM), F32),
            pltpu.VMEM((bsz, POOL_DIM), F32),
            pltpu.VMEM((bsz, DN_DIM), F32),
        ],
        compiler_params=pltpu.CompilerParams(dimension_semantics=("arbitrary",)),
        name="mixer_sample",
    )(h, s_all, cbuf_all, pbuf_all, *weights, s_out_prev)


def _pad_gate_param(v):
    out = jnp.zeros((v.shape[0], 1, GATE_PAD), F32)
    return out.at[:, 0, DECAY_LANE:DECAY_LANE + DN_HEADS].set(v.astype(F32))


def kernel(x_prompt, x_sample, state_delta, state_conv, state_pool, ln1_g, ln1_b, ffn1_w_gate, ffn1_w_up, ffn1_w_down, w_in, conv_w, a_log, dt_bias, onorm_g, pool_w, pool_scale, w_out, ln2_g, ln2_b, ffn2_w_gate, ffn2_w_up, ffn2_w_down, ln3_g, ln3_b):
    bsz, seq, _ = x_prompt.shape
    dec_b = x_sample.shape[0]

    win = w_in.astype(BF16)
    wout = w_out.astype(BF16)
    poolw = pool_w.astype(BF16)
    ffn1_stacks = (ffn1_w_gate, ffn1_w_up, ffn1_w_down)
    ffn2_stacks = (ffn2_w_gate, ffn2_w_up, ffn2_w_down)
    f1 = tuple(w[0].astype(BF16) for w in ffn1_stacks)
    alog = _pad_gate_param(a_log)
    dtb = _pad_gate_param(dt_bias)
    row = lambda v: v.reshape(DEPTH, 1, -1)
    ln1g, ln1b, ln2g, ln2b, ln3g, ln3b = map(row, (ln1_g, ln1_b, ln2_g, ln2_b, ln3_g, ln3_b))
    onorm = row(onorm_g)
    pscale = row(pool_scale)

    mix_w = (win, conv_w, alog, dtb, onorm, poolw, pscale, wout, ln2g, ln2b)
    xp = x_prompt.reshape(bsz * seq, D_MODEL)
    xs = x_sample.reshape(dec_b, D_MODEL)
    dp, cp, pp, cs, ps = [], [], [], [], []
    delta_sample = jnp.zeros(state_delta.shape, F32)
    for l in range(DEPTH):
        hp, hs = _ffn_ln(l, xp, xs, *f1, ln1g, ln1b)
        cast_jobs = [(w, l) for w in ffn2_stacks]
        if l + 1 < DEPTH:
            cast_jobs += [(w, l + 1) for w in ffn1_stacks]
        hp, s_p, c_p, p_p, *cast = _mixer_prompt(l, hp.reshape(bsz, seq, D_MODEL), mix_w, cast_jobs)
        f2, f1 = tuple(cast[:3]), tuple(cast[3:])
        hs, delta_sample, c_s, p_s = _mixer_sample(l, hs, state_delta, state_conv, state_pool, delta_sample, *mix_w)
        xp, xs = _ffn_ln(l, hp.reshape(bsz * seq, D_MODEL), hs, *f2, ln3g, ln3b)
        dp.append(s_p); cp.append(c_p); pp.append(p_p)
        cs.append(c_s); ps.append(p_s)
    return (xp.reshape(bsz, seq, D_MODEL), xs.reshape(dec_b, 1, D_MODEL),
            jnp.stack(dp), jnp.stack(cp), jnp.stack(pp), delta_sample, jnp.stack(cs), jnp.stack(ps))
```

```python
import functools

import jax
import jax.numpy as jnp
from jax import lax
from jax.experimental import pallas as pl
from jax.experimental.pallas import tpu as pltpu

F32 = jnp.float32
BF16 = jnp.bfloat16

D_MODEL = 1024
DEPTH = 4
DN_HEADS = 4
HEAD_DIM = 128
DN_DIM = DN_HEADS * HEAD_DIM
POOL_DIM = 512
QKV_DIM = 3 * DN_DIM
CONV_W = 4
POOL_WINDOWS = (2, 4, 8, 16)
POOL_GROUP_DIM = POOL_DIM // len(POOL_WINDOWS)
POOL_BUF = max(POOL_WINDOWS) - 1
D_FF = 2816
PAST_LEN = 16384
DN_ALPHA = (2.0 * DEPTH) ** 0.25
LN_EPS = 1e-5
RMS_EPS = 1e-6
L2_EPS = 1e-6

N_GATE = 2 * DN_HEADS
Z_OFF = QKV_DIM
GATE_OFF = QKV_DIM + DN_DIM
POOL_OFF = GATE_OFF + N_GATE
IN_DIM = POOL_OFF + POOL_DIM
GATE_PAD = 128
DECAY_LANE = DN_HEADS

CHUNK = 128
INV_BASE = 8
FFN_ROWS = 1024
FFN_SUB_ROWS = 256
MIX_ROWS = 512
OUT_CHUNKS = 2
CAST_STEPS = 2
BF16_SUBLANES = 16
SAMPLE_TILE = 8


def _dot(a, b):
    return jnp.dot(a.astype(BF16), b.astype(BF16), preferred_element_type=F32)


def _dot_nt(a, b):
    return lax.dot_general(a.astype(BF16), b.astype(BF16), (((1,), (1,)), ((), ())),
                           preferred_element_type=F32)


def _dot_tn(a, b):
    return lax.dot_general(a.astype(BF16), b.astype(BF16), (((0,), (0,)), ((), ())),
                           preferred_element_type=F32)


def _dot_exact_lhs(a_bf16, b):
    b1 = b.astype(BF16)
    r1 = b - b1.astype(F32)
    b2 = r1.astype(BF16)
    b3 = (r1 - b2.astype(F32)).astype(BF16)
    d = functools.partial(jnp.dot, preferred_element_type=F32)
    return d(a_bf16, b1) + (d(a_bf16, b2) + d(a_bf16, b3))


def _silu(x):
    return x * jax.nn.sigmoid(x)


def _softplus(x):
    return jnp.maximum(x, 0.0) + jnp.log1p(jnp.exp(-jnp.abs(x)))


def _layer_norm(y, g, b):
    mu = jnp.mean(y, axis=-1, keepdims=True)
    yc = y - mu
    var = jnp.mean(yc * yc, axis=-1, keepdims=True)
    return yc * lax.rsqrt(var + LN_EPS) * g + b


def _l2norm(x):
    return x * lax.rsqrt(jnp.sum(x * x, axis=-1, keepdims=True) + L2_EPS)


def _ffn_ln_rows(x, wg_ref, wu_ref, wd_ref, g_ref, b_ref):
    xb = x.astype(BF16)
    gate = jnp.dot(xb, wg_ref[...], preferred_element_type=F32)
    up = jnp.dot(xb, wu_ref[...], preferred_element_type=F32)
    act = (_silu(gate) * up).astype(BF16)
    ff = jnp.dot(act, wd_ref[...], preferred_element_type=F32)
    return _layer_norm(DN_ALPHA * x + 0.5 * ff, g_ref[...], b_ref[...])


def _ffn_ln_kernel(x_ref, xs_ref, wg_ref, wu_ref, wd_ref, g_ref, b_ref, o_ref, os_ref):
    weights = (wg_ref, wu_ref, wd_ref, g_ref, b_ref)
    tm = x_ref.shape[0]
    for r in range(tm // FFN_SUB_ROWS):
        rows = slice(r * FFN_SUB_ROWS, (r + 1) * FFN_SUB_ROWS)
        o_ref[rows, :] = _ffn_ln_rows(x_ref[rows, :], *weights)

    @pl.when(pl.program_id(0) == pl.num_programs(0) - 1)
    def _():
        os_ref[...] = _ffn_ln_rows(xs_ref[...], *weights)


def _resident(shape):
    zeros = (0,) * len(shape)
    return pl.BlockSpec(shape, lambda *_: zeros, pipeline_mode=pl.Buffered(1))


def _layer_resident(layer, shape):
    index = (layer,) + (0,) * len(shape)
    return pl.BlockSpec((None,) + tuple(shape), lambda *_: index, pipeline_mode=pl.Buffered(1))


def _ffn_ln(layer, x, xs, wg, wu, wd, g, b):
    rows = x.shape[0]
    tm = FFN_ROWS
    assert rows % tm == 0 and tm % FFN_SUB_ROWS == 0
    return pl.pallas_call(
        _ffn_ln_kernel,
        out_shape=(jax.ShapeDtypeStruct((rows, D_MODEL), F32), jax.ShapeDtypeStruct(xs.shape, F32)),
        grid=(rows // tm,),
        in_specs=[
            pl.BlockSpec((tm, D_MODEL), lambda i: (i, 0)),
            _resident(xs.shape),
            _resident((D_MODEL, D_FF)),
            _resident((D_MODEL, D_FF)),
            _resident((D_FF, D_MODEL)),
            _layer_resident(layer, (1, D_MODEL)),
            _layer_resident(layer, (1, D_MODEL)),
        ],
        out_specs=(pl.BlockSpec((tm, D_MODEL), lambda i: (i, 0)),
                   pl.BlockSpec(xs.shape, lambda i: (0, 0))),
        compiler_params=pltpu.CompilerParams(dimension_semantics=("arbitrary",)),
        name="ffn_ln",
    )(x, xs, wg, wu, wd, g, b)


def _gates(ba, alog, dtb):
    beta = jax.nn.sigmoid(ba)
    g = -jnp.exp(alog) * _softplus(ba + dtb)
    return beta, g


def _qkv_heads(conv_out):
    act = _silu(conv_out)
    qs, ks = [], []
    for h in range(DN_HEADS):
        lo = h * HEAD_DIM
        qs.append(_l2norm(act[:, lo:lo + HEAD_DIM]) * (HEAD_DIM ** -0.5))
        ks.append(_l2norm(act[:, DN_DIM + lo:DN_DIM + lo + HEAD_DIM]))
    return jnp.concatenate(qs, -1), jnp.concatenate(ks, -1), act[:, 2 * DN_DIM:]


def _gated_out_norm(o, z, onorm_g):
    outs = []
    for h in range(DN_HEADS):
        sl = slice(h * HEAD_DIM, (h + 1) * HEAD_DIM)
        oh = o[:, sl]
        oh = oh * lax.rsqrt(jnp.mean(oh * oh, axis=-1, keepdims=True) + RMS_EPS)
        outs.append(oh * onorm_g * _silu(z[:, sl]))
    return jnp.concatenate(outs, -1)


def _pool_project(d_groups, poolw_ref, pscale):
    ys = [_dot(d, poolw_ref[gi]) for gi, d in enumerate(d_groups)]
    return jnp.concatenate(ys, -1) * pscale


def _inv_unit_lower(ms, row, col, size):
    eye = jnp.where(row == col, 1.0, 0.0).astype(F32)
    base_bits = INV_BASE.bit_length() - 1
    base_mask = (row >> base_bits) == (col >> base_bits)
    ps = [jnp.where(base_mask, m, 0.0) for m in ms]
    xs = [eye - p for p in ps]
    ps = [_dot(p, p) for p in ps]
    for step in range(base_bits - 1):
        if step < base_bits - 2:
            xps = [_dot(jnp.concatenate([x, p], axis=0), p) for x, p in zip(xs, ps)]
            xs = [x + xp[:size] for x, xp in zip(xs, xps)]
            ps = [xp[size:] for xp in xps]
        else:
            xs = [x + _dot(x, p) for x, p in zip(xs, ps)]
    bits = base_bits
    while (1 << bits) < size:
        same_pair = (row >> (bits + 1)) == (col >> (bits + 1))
        lower_left = (((row >> bits) & 1) == 1) & (((col >> bits) & 1) == 0)
        mask = same_pair & lower_left
        ys = [_dot(jnp.where(mask, m, 0.0), x) for m, x in zip(ms, xs)]
        xs = [x - _dot(x, y) for x, y in zip(xs, ys)]
        bits += 1
    return xs


def _mixer_prompt_kernel(n_cast, h_ref, win_ref, convw_ref, alog_ref, dtb_ref, onorm_ref, poolw_ref, pscale_ref,
                         wout_ref, g2_ref, b2_ref, *refs):
    cast_src = refs[:n_cast]
    y_ref, sfin_ref, cfin_ref, pfin_ref = refs[n_cast:n_cast + 4]
    cast_dst = refs[n_cast + 4:2 * n_cast + 4]
    s_scr, ext_scr, pext_scr, qkv_scr, beta_scr, g_scr, z_scr, opool_scr, pw_scr, gw_scr = refs[2 * n_cast + 4:]
    tl = h_ref.shape[0]
    l = pl.program_id(1)
    n_l = pl.num_programs(1)

    @pl.when((pl.program_id(0) * n_l + l) % CAST_STEPS == 0)
    def _():
        for src, dst in zip(cast_src, cast_dst):
            dst[...] = src[...].astype(BF16)

    @pl.when(l == 0)
    def _():
        s_scr[...] = jnp.zeros_like(s_scr)
        ext_scr[:, 0:8, :] = jnp.zeros((QKV_DIM // HEAD_DIM, 8, HEAD_DIM), F32)
        pext_scr[:, 0:16, :] = jnp.zeros((len(POOL_WINDOWS), 16, POOL_GROUP_DIM), F32)

    hb = h_ref[...].astype(BF16)

    @pl.when((pl.program_id(0) == 0) & (l == 0))
    def _():
        pw_scr[...] = win_ref[:, POOL_OFF:POOL_OFF + POOL_DIM]
        gw_scr[...] = jnp.concatenate([win_ref[:, GATE_OFF:GATE_OFF + N_GATE],
                                       jnp.zeros((D_MODEL, GATE_PAD - N_GATE), BF16)], axis=-1)

    def project(w_ref, lo, width):
        return jnp.dot(hb, w_ref[:, lo:lo + width], preferred_element_type=F32)

    def conv_slab(t, x_t):
        lanes = slice(t * HEAD_DIM, (t + 1) * HEAD_DIM)
        ext_scr[t, 8:8 + tl, :] = x_t
        conv = convw_ref[CONV_W - 1:CONV_W, lanes] * x_t
        for i in range(CONV_W - 1):
            conv = conv + convw_ref[i:i + 1, lanes] * ext_scr[t, 5 + i:5 + i + tl, :]
        cfin_ref[:, lanes] = ext_scr[t, tl + 5:tl + 8, :]
        ext_scr[t, 0:8, :] = ext_scr[t, tl:tl + 8, :]
        act = _silu(conv)
        if t < DN_HEADS:
            act = _l2norm(act) * (HEAD_DIM ** -0.5)
        elif t < 2 * DN_HEADS:
            act = _l2norm(act)
        qkv_scr[t] = act

    def pool_slab(gi, tok):
        w = POOL_WINDOWS[gi]
        lanes = slice(gi * POOL_GROUP_DIM, (gi + 1) * POOL_GROUP_DIM)
        pext_scr[gi, 16:16 + tl, :] = tok
        s = tok
        for i in range(1, w):
            s = s + pext_scr[gi, 16 - i:16 - i + tl, :]
        pos = l * tl + lax.broadcasted_iota(jnp.int32, (tl, 1), 0)
        cnt = jnp.minimum(w, pos + 1).astype(F32)
        pfin_ref[:, lanes] = pext_scr[gi, tl + 1:tl + 16, :]
        pext_scr[gi, 0:16, :] = pext_scr[gi, tl:tl + 16, :]
        return s / cnt - tok

    group = 2 * HEAD_DIM
    for j in range(QKV_DIM // group):
        y = project(win_ref, j * group, group)
        conv_slab(2 * j, y[:, :HEAD_DIM])
        conv_slab(2 * j + 1, y[:, HEAD_DIM:])
    beta_t, g_t = _gates(project(gw_scr, 0, GATE_PAD), alog_ref[...], dtb_ref[...])
    beta_scr[...] = beta_t
    g_scr[...] = g_t
    d_groups = []
    for j in range(POOL_DIM // group):
        y = project(pw_scr, j * group, group)
        d_groups.append(pool_slab(2 * j, y[:, :POOL_GROUP_DIM]))
        d_groups.append(pool_slab(2 * j + 1, y[:, POOL_GROUP_DIM:]))
    opool_scr[...] = _pool_project(d_groups, poolw_ref, pscale_ref[...])
    z_scr[...] = project(win_ref, Z_OFF, DN_DIM)

    c = CHUNK
    row = lax.broadcasted_iota(jnp.int32, (c, c), 0)
    col = lax.broadcasted_iota(jnp.int32, (c, c), 1)
    incl = row >= col
    strict = row > col
    tri = jnp.where(incl, 1.0, 0.0).astype(BF16)

    n_chunks = tl // c
    chains = [(ci, hd) for ci in range(n_chunks) for hd in range(DN_HEADS)]
    gc_alls, gc_ts, beta_alls = [], [], []
    for ci in range(n_chunks):
        rows = slice(ci * c, (ci + 1) * c)
        gc_all = _dot_exact_lhs(tri, g_scr[rows, :])
        gc_alls.append(gc_all)
        gc_ts.append(gc_all.T)
        beta_alls.append(beta_scr[rows, :])
    qs, ks, betas, egs, decs, rhss, k_ends, g_ends = [], [], [], [], [], [], [], []
    for ci, hd in chains:
        rows = slice(ci * c, (ci + 1) * c)
        sl = slice(hd * HEAD_DIM, (hd + 1) * HEAD_DIM)
        lane = DECAY_LANE + hd
        qh, kh, vh = qkv_scr[hd, rows, :], qkv_scr[DN_HEADS + hd, rows, :], qkv_scr[2 * DN_HEADS + hd, rows, :]
        beta = beta_alls[ci][:, hd:hd + 1]
        gc_col = gc_alls[ci][:, lane:lane + 1]
        gc_row = gc_ts[ci][lane:lane + 1, :]
        gc_last = gc_alls[ci][c - 1:c, lane:lane + 1]
        eg = jnp.exp(gc_col)
        decs.append(jnp.where(incl, jnp.exp(jnp.where(incl, gc_col - gc_row, 0.0)), 0.0))
        rhss.append(jnp.concatenate([beta * vh, (beta * eg) * kh], axis=-1))
        k_ends.append(kh * jnp.exp(gc_last - gc_col))
        g_ends.append(jnp.exp(gc_last))
        qs.append(qh); ks.append(kh); betas.append(beta); egs.append(eg)
    kqs = [_dot_nt(jnp.concatenate([kh, qh], axis=0), kh) for kh, qh in zip(ks, qs)]
    ms = [jnp.where(strict, beta * kq[:c] * dec, 0.0) for beta, kq, dec in zip(betas, kqs, decs)]
    qks = [kq[c:] * dec for kq, dec in zip(kqs, decs)]
    t_invs = _inv_unit_lower(ms, row, col, c)
    sols = [_dot(t_inv, rhs) for t_inv, rhs in zip(t_invs, rhss)]
    q_decs = [qh * eg for qh, eg in zip(qs, egs)]

    s_cur = [s_scr[hd] for hd in range(DN_HEADS)]
    o_chunks = []
    for ci in range(n_chunks):
        idx = [ci * DN_HEADS + hd for hd in range(DN_HEADS)]
        wss = [_dot(jnp.concatenate([sols[i][:, HEAD_DIM:], q_decs[i]], axis=0), s) for i, s in zip(idx, s_cur)]
        us = [sols[i][:, :HEAD_DIM] - ws[:c] for i, ws in zip(idx, wss)]
        outs = [ws[c:] + _dot(qks[i], u) for i, ws, u in zip(idx, wss, us)]
        s_cur = [g_ends[i] * s + _dot_tn(k_ends[i], u) for i, s, u in zip(idx, s_cur, us)]
        o_chunks.append(jnp.concatenate(outs, axis=-1))
        if len(o_chunks) == OUT_CHUNKS or ci == n_chunks - 1:
            rows = slice((ci + 1 - len(o_chunks)) * c, (ci + 1) * c)
            o_dn = _gated_out_norm(jnp.concatenate(o_chunks, axis=0), z_scr[rows, :], onorm_ref[...])
            mix = _dot(jnp.concatenate([o_dn, opool_scr[rows, :]], axis=-1), wout_ref[...])
            y_ref[rows, :] = _layer_norm(DN_ALPHA * h_ref[rows, :] + mix, g2_ref[...], b2_ref[...])
            o_chunks = []
    for hd in range(DN_HEADS):
        s_scr[hd] = s_cur[hd]

    @pl.when(l == n_l - 1)
    def _():
        sfin_ref[...] = s_scr[...]


def _mixer_weight_specs(layer):
    return [
        _resident((D_MODEL, IN_DIM)),
        _layer_resident(layer, (CONV_W, QKV_DIM)),
        _layer_resident(layer, (1, GATE_PAD)),
        _layer_resident(layer, (1, GATE_PAD)),
        _layer_resident(layer, (1, HEAD_DIM)),
        _layer_resident(layer, (len(POOL_WINDOWS), POOL_GROUP_DIM, POOL_GROUP_DIM)),
        _layer_resident(layer, (1, POOL_DIM)),
        _resident((DN_DIM + POOL_DIM, D_MODEL)),
        _layer_resident(layer, (1, D_MODEL)),
        _layer_resident(layer, (1, D_MODEL)),
    ]


def _mixer_prompt(layer, h, weights, cast_jobs):
    bsz, seq, _ = h.shape
    tl = MIX_ROWS
    n_l = seq // tl
    assert seq % tl == 0 and tl % CHUNK == 0 and (bsz * n_l) % CAST_STEPS == 0
    n_blocks = bsz * n_l // CAST_STEPS
    out_shape = [
        jax.ShapeDtypeStruct((bsz, seq, D_MODEL), F32),
        jax.ShapeDtypeStruct((bsz, DN_HEADS, HEAD_DIM, HEAD_DIM), F32),
        jax.ShapeDtypeStruct((bsz, CONV_W - 1, QKV_DIM), F32),
        jax.ShapeDtypeStruct((bsz, POOL_BUF, POOL_DIM), F32),
    ]
    out_specs = [
        pl.BlockSpec((None, tl, D_MODEL), lambda b, l: (b, l, 0)),
        pl.BlockSpec((None, DN_HEADS, HEAD_DIM, HEAD_DIM), lambda b, l: (b, 0, 0, 0)),
        pl.BlockSpec((None, CONV_W - 1, QKV_DIM), lambda b, l: (b, 0, 0)),
        pl.BlockSpec((None, POOL_BUF, POOL_DIM), lambda b, l: (b, 0, 0)),
    ]
    cast_specs = []
    for w_all, w_layer in cast_jobs:
        _, n_rows, n_cols = w_all.shape
        assert n_rows % (n_blocks * BF16_SUBLANES) == 0
        blk = n_rows // n_blocks
        cast_specs.append(pl.BlockSpec((None, blk, n_cols),
                                       lambda b, l, w_layer=w_layer: (w_layer, (b * n_l + l) // CAST_STEPS, 0)))
        out_shape.append(jax.ShapeDtypeStruct((n_rows, n_cols), BF16))
        out_specs.append(pl.BlockSpec((blk, n_cols), lambda b, l: ((b * n_l + l) // CAST_STEPS, 0)))
    return pl.pallas_call(
        functools.partial(_mixer_prompt_kernel, len(cast_jobs)),
        out_shape=tuple(out_shape),
        grid=(bsz, n_l),
        in_specs=[
            pl.BlockSpec((None, tl, D_MODEL), lambda b, l: (b, l, 0)),
        ] + _mixer_weight_specs(layer) + cast_specs,
        out_specs=tuple(out_specs),
        scratch_shapes=[
            pltpu.VMEM((DN_HEADS, HEAD_DIM, HEAD_DIM), F32),
            pltpu.VMEM((QKV_DIM // HEAD_DIM, tl + 8, HEAD_DIM), F32),
            pltpu.VMEM((len(POOL_WINDOWS), tl + 16, POOL_GROUP_DIM), F32),
            pltpu.VMEM((QKV_DIM // HEAD_DIM, tl, HEAD_DIM), F32),
            pltpu.VMEM((tl, GATE_PAD), F32),
            pltpu.VMEM((tl, GATE_PAD), F32),
            pltpu.VMEM((tl, DN_DIM), F32),
            pltpu.VMEM((tl, POOL_DIM), F32),
            pltpu.VMEM((D_MODEL, POOL_DIM), BF16),
            pltpu.VMEM((D_MODEL, GATE_PAD), BF16),
        ],
        compiler_params=pltpu.CompilerParams(dimension_semantics=("arbitrary", "arbitrary")),
        name="mixer_prompt",
    )(h, *weights, *[w_all for w_all, _ in cast_jobs])


N_SAMPLE_STATES = 3

def _mixer_sample_kernel(h_ref, s_ref, cbuf_ref, pbuf_ref, win_ref, convw_ref, alog_ref, dtb_ref, onorm_ref,
                         poolw_ref, pscale_ref, wout_ref, g2_ref, b2_ref, *refs):
    refs = refs[N_SAMPLE_STATES:]
    y_ref, snew_ref, cnew_ref, pnew_ref = refs[:4]
    q_scr, k_scr, v_scr, beta_scr, eg_scr, z_scr, opool_scr, o_scr = refs[4:]
    i = pl.program_id(0)
    n_i = pl.num_programs(0)

    @pl.when(i == 0)
    def _():
        proj = jnp.dot(h_ref[...].astype(BF16), win_ref[...], preferred_element_type=F32)
        qkv = proj[:, 0:QKV_DIM]
        conv = convw_ref[CONV_W - 1:CONV_W, :] * qkv
        for j in range(CONV_W - 1):
            conv = conv + convw_ref[j:j + 1, :] * cbuf_ref[j]
        for j in range(CONV_W - 2):
            cnew_ref[j] = cbuf_ref[j + 1]
        cnew_ref[CONV_W - 2] = qkv
        q, k, v = _qkv_heads(conv)
        q_scr[...] = q
        k_scr[...] = k
        v_scr[...] = v
        ba = jnp.concatenate([proj[:, GATE_OFF:GATE_OFF + N_GATE],
                              jnp.zeros((proj.shape[0], GATE_PAD - N_GATE), F32)], axis=-1)
        beta_t, g_t = _gates(ba, alog_ref[...], dtb_ref[...])
        beta_scr[...] = beta_t
        eg_scr[...] = jnp.exp(g_t)
        z_scr[...] = proj[:, Z_OFF:Z_OFF + DN_DIM]

        p = proj[:, POOL_OFF:POOL_OFF + POOL_DIM]
        d_groups = []
        for gi, w in enumerate(POOL_WINDOWS):
            lo = gi * POOL_GROUP_DIM
            tok = p[:, lo:lo + POOL_GROUP_DIM]
            s = tok
            for back in range(1, w):
                s = s + pbuf_ref[POOL_BUF - back, :, lo:lo + POOL_GROUP_DIM]
            cnt = float(min(w, PAST_LEN + 1))
            d_groups.append(s / cnt - tok)
        opool_scr[...] = _pool_project(d_groups, poolw_ref, pscale_ref[...])
        for j in range(POOL_BUF - 1):
            pnew_ref[j] = pbuf_ref[j + 1]
        pnew_ref[POOL_BUF - 1] = p

    rows = pl.ds(pl.multiple_of(i * SAMPLE_TILE, SAMPLE_TILE), SAMPLE_TILE)
    q_t, k_t, v_t = q_scr[rows, :], k_scr[rows, :], v_scr[rows, :]
    beta_t, a_t = beta_scr[rows, :], eg_scr[rows, :]
    eye = jnp.where(lax.broadcasted_iota(jnp.int32, (HEAD_DIM, HEAD_DIM), 0)
                    == lax.broadcasted_iota(jnp.int32, (HEAD_DIM, HEAD_DIM), 1), 1.0, 0.0).astype(F32)
    o_heads = [[] for _ in range(SAMPLE_TILE)]
    for hd in range(DN_HEADS):
        sl = slice(hd * HEAD_DIM, (hd + 1) * HEAD_DIM)
        for j in range(SAMPLE_TILE):
            qh = q_t[j:j + 1, sl]
            kh = k_t[j:j + 1, sl]
            vh = v_t[j:j + 1, sl]
            beta = beta_t[j:j + 1, hd:hd + 1]
            a = a_t[j:j + 1, DECAY_LANE + hd:DECAY_LANE + hd + 1]
            k_col = jnp.sum(eye * kh, axis=-1, keepdims=True)
            q_col = jnp.sum(eye * qh, axis=-1, keepdims=True)
            s_old = s_ref[j, hd]
            ks = jnp.sum(k_col * s_old, axis=0, keepdims=True)
            qs = jnp.sum(q_col * s_old, axis=0, keepdims=True)
            u = beta * vh - (beta * a) * ks
            qk = jnp.sum(qh * kh, axis=-1, keepdims=True)
            o_heads[j].append(a * qs + qk * u)
            snew_ref[j, hd] = a * s_old + k_col * u
    o_scr[rows, :] = jnp.concatenate([jnp.concatenate(o_h, axis=-1) for o_h in o_heads], axis=0)

    @pl.when(i == n_i - 1)
    def _():
        o_dn = _gated_out_norm(o_scr[...], z_scr[...], onorm_ref[...])
        mix = _dot(jnp.concatenate([o_dn, opool_scr[...]], axis=-1), wout_ref[...])
        y_ref[...] = _layer_norm(DN_ALPHA * h_ref[...] + mix, g2_ref[...], b2_ref[...])


def _mixer_sample(layer, h, s_all, cbuf_all, pbuf_all, new_states, *weights):
    bsz = h.shape[0]
    assert bsz % SAMPLE_TILE == 0 and len(new_states) == N_SAMPLE_STATES
    cshape = (CONV_W - 1, bsz, QKV_DIM)
    pshape = (POOL_BUF, bsz, POOL_DIM)
    assert cbuf_all.shape[1:] == cshape and pbuf_all.shape[1:] == pshape
    out_shape = (
        jax.ShapeDtypeStruct((bsz, D_MODEL), F32),
        jax.ShapeDtypeStruct(s_all.shape, F32),
        jax.ShapeDtypeStruct(cbuf_all.shape, F32),
        jax.ShapeDtypeStruct(pbuf_all.shape, F32),
    )
    state_spec = pl.BlockSpec((None, SAMPLE_TILE, DN_HEADS, HEAD_DIM, HEAD_DIM), lambda i: (layer, i, 0, 0, 0))
    n_in = 4 + 10
    return pl.pallas_call(
        _mixer_sample_kernel,
        out_shape=out_shape,
        grid=(bsz // SAMPLE_TILE,),
        in_specs=[
            _resident((bsz, D_MODEL)),
            state_spec,
            _layer_resident(layer, cshape),
            _layer_resident(layer, pshape),
        ] + _mixer_weight_specs(layer) + [pl.BlockSpec(memory_space=pl.ANY)] * len(new_states),
        out_specs=(
            pl.BlockSpec((bsz, D_MODEL), lambda i: (0, 0)),
            state_spec,
            pl.BlockSpec((None,) + cshape, lambda i: (layer, 0, 0, 0)),
            pl.BlockSpec((None,) + pshape, lambda i: (layer, 0, 0, 0)),
        ),
        input_output_aliases={n_in + k: 1 + k for k in range(len(new_states))},
        scratch_shapes=[
            pltpu.VMEM((bsz, DN_DIM), F32),
            pltpu.VMEM((bsz, DN_DIM), F32),
            pltpu.VMEM((bsz, DN_DIM), F32),
            pltpu.VMEM((bsz, GATE_PAD), F32),
            pltpu.VMEM((bsz, GATE_PAD), F32),
            pltpu.VMEM((bsz, DN_DIM), F32),
            pltpu.VMEM((bsz, POOL_DIM), F32),
            pltpu.VMEM((bsz, DN_DIM), F32),
        ],
        compiler_params=pltpu.CompilerParams(dimension_semantics=("arbitrary",)),
        name="mixer_sample",
    )(h, s_all, cbuf_all, pbuf_all, *weights, *new_states)


def _pad_gate_param(v):
    out = jnp.zeros((v.shape[0], 1, GATE_PAD), F32)
    return out.at[:, 0, DECAY_LANE:DECAY_LANE + DN_HEADS].set(v.astype(F32))


def kernel(x_prompt, x_sample, state_delta, state_conv, state_pool, ln1_g, ln1_b, ffn1_w_gate, ffn1_w_up, ffn1_w_down, w_in, conv_w, a_log, dt_bias, onorm_g, pool_w, pool_scale, w_out, ln2_g, ln2_b, ffn2_w_gate, ffn2_w_up, ffn2_w_down, ln3_g, ln3_b):
    bsz, seq, _ = x_prompt.shape
    dec_b = x_sample.shape[0]

    ffn1_stacks = (ffn1_w_gate, ffn1_w_up, ffn1_w_down)
    ffn2_stacks = (ffn2_w_gate, ffn2_w_up, ffn2_w_down)
    mix_stacks = (w_in, w_out)
    f1 = tuple(w[0].astype(BF16) for w in ffn1_stacks)
    win, wout = (w[0].astype(BF16) for w in mix_stacks)
    poolw = pool_w.astype(BF16)
    alog = _pad_gate_param(a_log)
    dtb = _pad_gate_param(dt_bias)
    row = lambda v: v.reshape(DEPTH, 1, -1)
    ln1g, ln1b, ln2g, ln2b, ln3g, ln3b = map(row, (ln1_g, ln1_b, ln2_g, ln2_b, ln3_g, ln3_b))
    onorm = row(onorm_g)
    pscale = row(pool_scale)

    xp = x_prompt.reshape(bsz * seq, D_MODEL)
    xs = x_sample.reshape(dec_b, D_MODEL)
    dp, cp, pp = [], [], []
    cbuf_all = jnp.transpose(state_conv, (0, 2, 1, 3))
    pbuf_all = jnp.transpose(state_pool, (0, 2, 1, 3))
    sample_states = tuple(jnp.zeros(s.shape, F32) for s in (state_delta, cbuf_all, pbuf_all))
    for l in range(DEPTH):
        hp, hs = _ffn_ln(l, xp, xs, *f1, ln1g, ln1b)
        mix_w = (win, conv_w, alog, dtb, onorm, poolw, pscale, wout, ln2g, ln2b)
        cast_jobs = [(w, l) for w in ffn2_stacks]
        if l + 1 < DEPTH:
            cast_jobs += [(w, l + 1) for w in ffn1_stacks + mix_stacks]
        hp, s_p, c_p, p_p, *cast = _mixer_prompt(l, hp.reshape(bsz, seq, D_MODEL), mix_w, cast_jobs)
        f2, f1, (win, wout) = tuple(cast[:3]), tuple(cast[3:6]), cast[6:] or (None, None)
        hs, *sample_states = _mixer_sample(l, hs, state_delta, cbuf_all, pbuf_all, tuple(sample_states), *mix_w)
        xp, xs = _ffn_ln(l, hp.reshape(bsz * seq, D_MODEL), hs, *f2, ln3g, ln3b)
        dp.append(s_p); cp.append(c_p); pp.append(p_p)
    delta_sample, conv_sample, pool_sample = sample_states
    return (xp.reshape(bsz, seq, D_MODEL), xs.reshape(dec_b, 1, D_MODEL),
            jnp.stack(dp), jnp.stack(cp), jnp.stack(pp),
            delta_sample, jnp.transpose(conv_sample, (0, 2, 1, 3)), jnp.transpose(pool_sample, (0, 2, 1, 3)))
```

```python
import functools

import jax
import jax.numpy as jnp
from jax import lax
from jax.experimental import pallas as pl
from jax.experimental.pallas import tpu as pltpu

F32 = jnp.float32
BF16 = jnp.bfloat16

D_MODEL = 1024
DEPTH = 4
DN_HEADS = 4
HEAD_DIM = 128
DN_DIM = DN_HEADS * HEAD_DIM
POOL_DIM = 512
QKV_DIM = 3 * DN_DIM
CONV_W = 4
POOL_WINDOWS = (2, 4, 8, 16)
POOL_GROUP_DIM = POOL_DIM // len(POOL_WINDOWS)
POOL_BUF = max(POOL_WINDOWS) - 1
D_FF = 2816
PAST_LEN = 16384
DN_ALPHA = (2.0 * DEPTH) ** 0.25
LN_EPS = 1e-5
RMS_EPS = 1e-6
L2_EPS = 1e-6

N_GATE = 2 * DN_HEADS
Z_OFF = QKV_DIM
GATE_OFF = QKV_DIM + DN_DIM
POOL_OFF = GATE_OFF + N_GATE
IN_DIM = POOL_OFF + POOL_DIM
GATE_PAD = 128
DECAY_LANE = DN_HEADS

CHUNK = 128
INV_BASE = 8
FFN_ROWS = 1024
FFN_SUB_ROWS = 256
MIX_ROWS = 512
OUT_CHUNKS = 2
CAST_STEPS = 2
BF16_SUBLANES = 16
F32_SUBLANES = 8
SAMPLE_TILE = 8


def _dot(a, b):
    return jnp.dot(a.astype(BF16), b.astype(BF16), preferred_element_type=F32)


def _dot_nt(a, b):
    return lax.dot_general(a.astype(BF16), b.astype(BF16), (((1,), (1,)), ((), ())),
                           preferred_element_type=F32)


def _dot_tn(a, b):
    return lax.dot_general(a.astype(BF16), b.astype(BF16), (((0,), (0,)), ((), ())),
                           preferred_element_type=F32)


def _split_bf16(x):
    hi = x.astype(BF16)
    lo = (x - hi.astype(F32)).astype(BF16)
    return hi, lo


def _dot_exact_lhs(a_bf16, b):
    b1 = b.astype(BF16)
    r1 = b - b1.astype(F32)
    b2 = r1.astype(BF16)
    b3 = (r1 - b2.astype(F32)).astype(BF16)
    d = functools.partial(jnp.dot, preferred_element_type=F32)
    return d(a_bf16, b1) + (d(a_bf16, b2) + d(a_bf16, b3))


def _silu(x):
    return x * jax.nn.sigmoid(x)


def _softplus(x):
    return jnp.maximum(x, 0.0) + jnp.log1p(jnp.exp(-jnp.abs(x)))


def _layer_norm(y, g, b):
    mu = jnp.mean(y, axis=-1, keepdims=True)
    yc = y - mu
    var = jnp.mean(yc * yc, axis=-1, keepdims=True)
    return yc * lax.rsqrt(var + LN_EPS) * g + b


def _l2norm(x):
    return x * lax.rsqrt(jnp.sum(x * x, axis=-1, keepdims=True) + L2_EPS)


def _ffn_ln_rows(x, wg_ref, wu_ref, wd_ref, g_ref, b_ref):
    xb = x.astype(BF16)
    gate = jnp.dot(xb, wg_ref[...], preferred_element_type=F32)
    up = jnp.dot(xb, wu_ref[...], preferred_element_type=F32)
    act = (_silu(gate) * up).astype(BF16)
    ff = jnp.dot(act, wd_ref[...], preferred_element_type=F32)
    return _layer_norm(DN_ALPHA * x + 0.5 * ff, g_ref[...], b_ref[...])


def _ffn_ln_kernel(x_ref, xs_ref, wg_ref, wu_ref, wd_ref, g_ref, b_ref, o_ref, os_ref):
    weights = (wg_ref, wu_ref, wd_ref, g_ref, b_ref)
    tm = x_ref.shape[0]
    for r in range(tm // FFN_SUB_ROWS):
        rows = slice(r * FFN_SUB_ROWS, (r + 1) * FFN_SUB_ROWS)
        o_ref[rows, :] = _ffn_ln_rows(x_ref[rows, :], *weights)

    @pl.when(pl.program_id(0) == pl.num_programs(0) - 1)
    def _():
        os_ref[...] = _ffn_ln_rows(xs_ref[...], *weights)


def _resident(shape):
    zeros = (0,) * len(shape)
    return pl.BlockSpec(shape, lambda *_: zeros, pipeline_mode=pl.Buffered(1))


def _layer_resident(layer, shape):
    index = (layer,) + (0,) * len(shape)
    return pl.BlockSpec((None,) + tuple(shape), lambda *_: index, pipeline_mode=pl.Buffered(1))


def _ffn_ln(layer, x, xs, wg, wu, wd, g, b):
    rows = x.shape[0]
    tm = FFN_ROWS
    assert rows % tm == 0 and tm % FFN_SUB_ROWS == 0
    return pl.pallas_call(
        _ffn_ln_kernel,
        out_shape=(jax.ShapeDtypeStruct((rows, D_MODEL), F32), jax.ShapeDtypeStruct(xs.shape, F32)),
        grid=(rows // tm,),
        in_specs=[
            pl.BlockSpec((tm, D_MODEL), lambda i: (i, 0)),
            _resident(xs.shape),
            _resident((D_MODEL, D_FF)),
            _resident((D_MODEL, D_FF)),
            _resident((D_FF, D_MODEL)),
            _layer_resident(layer, (1, D_MODEL)),
            _layer_resident(layer, (1, D_MODEL)),
        ],
        out_specs=(pl.BlockSpec((tm, D_MODEL), lambda i: (i, 0)),
                   pl.BlockSpec(xs.shape, lambda i: (0, 0))),
        compiler_params=pltpu.CompilerParams(dimension_semantics=("arbitrary",)),
        name="ffn_ln",
    )(x, xs, wg, wu, wd, g, b)


def _gates(ba, alog, dtb):
    beta = jax.nn.sigmoid(ba)
    g = -jnp.exp(alog) * _softplus(ba + dtb)
    return beta, g


def _qkv_heads(conv_out):
    act = _silu(conv_out)
    qs, ks = [], []
    for h in range(DN_HEADS):
        lo = h * HEAD_DIM
        qs.append(_l2norm(act[:, lo:lo + HEAD_DIM]) * (HEAD_DIM ** -0.5))
        ks.append(_l2norm(act[:, DN_DIM + lo:DN_DIM + lo + HEAD_DIM]))
    return jnp.concatenate(qs, -1), jnp.concatenate(ks, -1), act[:, 2 * DN_DIM:]


def _gated_out_norm(o, z, onorm_g):
    outs = []
    for h in range(DN_HEADS):
        sl = slice(h * HEAD_DIM, (h + 1) * HEAD_DIM)
        oh = o[:, sl]
        oh = oh * lax.rsqrt(jnp.mean(oh * oh, axis=-1, keepdims=True) + RMS_EPS)
        outs.append(oh * onorm_g * _silu(z[:, sl]))
    return jnp.concatenate(outs, -1)


def _pool_project(d_groups, poolw_ref, pscale):
    ys = [_dot(d, poolw_ref[gi]) for gi, d in enumerate(d_groups)]
    return jnp.concatenate(ys, -1) * pscale


def _inv_unit_lower(ms, row, col, size):
    eye = jnp.where(row == col, 1.0, 0.0).astype(F32)
    base_bits = INV_BASE.bit_length() - 1
    base_mask = (row >> base_bits) == (col >> base_bits)
    ps = [jnp.where(base_mask, m, 0.0) for m in ms]
    xs = [eye - p for p in ps]
    ps = [_dot(p, p) for p in ps]
    for step in range(base_bits - 1):
        if step < base_bits - 2:
            xps = [_dot(jnp.concatenate([x, p], axis=0), p) for x, p in zip(xs, ps)]
            xs = [x + xp[:size] for x, xp in zip(xs, xps)]
            ps = [xp[size:] for xp in xps]
        else:
            xs = [x + _dot(x, p) for x, p in zip(xs, ps)]
    bits = base_bits
    while (1 << bits) < size:
        same_pair = (row >> (bits + 1)) == (col >> (bits + 1))
        lower_left = (((row >> bits) & 1) == 1) & (((col >> bits) & 1) == 0)
        mask = same_pair & lower_left
        ys = [_dot(jnp.where(mask, m, 0.0), x) for m, x in zip(ms, xs)]
        xs = [x - _dot(x, y) for x, y in zip(xs, ys)]
        bits += 1
    return xs


def _mixer_prompt_kernel(n_cast, h_ref, win_ref, convw_ref, alog_ref, dtb_ref, onorm_ref, poolw_ref, pscale_ref,
                         wout_ref, g2_ref, b2_ref, *refs):
    cast_src = refs[:n_cast]
    y_ref, sfin_ref, cfin_ref, pfin_ref = refs[n_cast:n_cast + 4]
    cast_dst = refs[n_cast + 4:2 * n_cast + 4]
    s_scr, ext_scr, pext_scr, qkv_scr, beta_scr, g_scr, z_scr, opool_scr, pw_scr, gw_scr = refs[2 * n_cast + 4:]
    tl = h_ref.shape[0]
    l = pl.program_id(1)
    n_l = pl.num_programs(1)

    @pl.when((pl.program_id(0) * n_l + l) % CAST_STEPS == 0)
    def _():
        for src, dst in zip(cast_src, cast_dst):
            dst[...] = src[...].astype(BF16)

    @pl.when(l == 0)
    def _():
        s_scr[...] = jnp.zeros_like(s_scr)
        ext_scr[:, 0:8, :] = jnp.zeros((QKV_DIM // HEAD_DIM, 8, HEAD_DIM), F32)
        pext_scr[:, 0:16, :] = jnp.zeros((len(POOL_WINDOWS), 16, POOL_GROUP_DIM), F32)

    hb = h_ref[...].astype(BF16)

    @pl.when((pl.program_id(0) == 0) & (l == 0))
    def _():
        pw_scr[...] = win_ref[:, POOL_OFF:POOL_OFF + POOL_DIM]
        gw_scr[...] = jnp.concatenate([win_ref[:, GATE_OFF:GATE_OFF + N_GATE],
                                       jnp.zeros((D_MODEL, GATE_PAD - N_GATE), BF16)], axis=-1)

    def project(w_ref, lo, width):
        return jnp.dot(hb, w_ref[:, lo:lo + width], preferred_element_type=F32)

    def conv_slab(t, x_t):
        lanes = slice(t * HEAD_DIM, (t + 1) * HEAD_DIM)
        ext_scr[t, 8:8 + tl, :] = x_t
        conv = convw_ref[CONV_W - 1:CONV_W, lanes] * x_t
        for i in range(CONV_W - 1):
            conv = conv + convw_ref[i:i + 1, lanes] * ext_scr[t, 5 + i:5 + i + tl, :]
        cfin_ref[:, lanes] = ext_scr[t, tl + 5:tl + 8, :]
        ext_scr[t, 0:8, :] = ext_scr[t, tl:tl + 8, :]
        act = _silu(conv)
        if t < DN_HEADS:
            act = _l2norm(act) * (HEAD_DIM ** -0.5)
        elif t < 2 * DN_HEADS:
            act = _l2norm(act)
        qkv_scr[t] = act

    def pool_slab(gi, tok):
        w = POOL_WINDOWS[gi]
        lanes = slice(gi * POOL_GROUP_DIM, (gi + 1) * POOL_GROUP_DIM)
        pext_scr[gi, 16:16 + tl, :] = tok
        s = tok
        for i in range(1, w):
            s = s + pext_scr[gi, 16 - i:16 - i + tl, :]
        pos = l * tl + lax.broadcasted_iota(jnp.int32, (tl, 1), 0)
        cnt = jnp.minimum(w, pos + 1).astype(F32)
        pfin_ref[:, lanes] = pext_scr[gi, tl + 1:tl + 16, :]
        pext_scr[gi, 0:16, :] = pext_scr[gi, tl:tl + 16, :]
        return s / cnt - tok

    group = 2 * HEAD_DIM
    for j in range(QKV_DIM // group):
        y = project(win_ref, j * group, group)
        conv_slab(2 * j, y[:, :HEAD_DIM])
        conv_slab(2 * j + 1, y[:, HEAD_DIM:])
    beta_t, g_t = _gates(project(gw_scr, 0, GATE_PAD), alog_ref[...], dtb_ref[...])
    beta_scr[...] = beta_t
    g_scr[...] = g_t
    d_groups = []
    for j in range(POOL_DIM // group):
        y = project(pw_scr, j * group, group)
        d_groups.append(pool_slab(2 * j, y[:, :POOL_GROUP_DIM]))
        d_groups.append(pool_slab(2 * j + 1, y[:, POOL_GROUP_DIM:]))
    opool_scr[...] = _pool_project(d_groups, poolw_ref, pscale_ref[...])
    z_scr[...] = project(win_ref, Z_OFF, DN_DIM)

    c = CHUNK
    row = lax.broadcasted_iota(jnp.int32, (c, c), 0)
    col = lax.broadcasted_iota(jnp.int32, (c, c), 1)
    incl = row >= col
    strict = row > col
    tri = jnp.where(incl, 1.0, 0.0).astype(BF16)

    n_chunks = tl // c
    chains = [(ci, hd) for ci in range(n_chunks) for hd in range(DN_HEADS)]
    gc_alls, gc_ts, beta_alls = [], [], []
    for ci in range(n_chunks):
        rows = slice(ci * c, (ci + 1) * c)
        gc_all = _dot_exact_lhs(tri, g_scr[rows, :])
        gc_alls.append(gc_all)
        gc_ts.append(gc_all.T)
        beta_alls.append(beta_scr[rows, :])
    qs, ks, betas, egs, decs, rhss, k_ends, g_ends = [], [], [], [], [], [], [], []
    for ci, hd in chains:
        rows = slice(ci * c, (ci + 1) * c)
        sl = slice(hd * HEAD_DIM, (hd + 1) * HEAD_DIM)
        lane = DECAY_LANE + hd
        qh, kh, vh = qkv_scr[hd, rows, :], qkv_scr[DN_HEADS + hd, rows, :], qkv_scr[2 * DN_HEADS + hd, rows, :]
        beta = beta_alls[ci][:, hd:hd + 1]
        gc_col = gc_alls[ci][:, lane:lane + 1]
        gc_row = gc_ts[ci][lane:lane + 1, :]
        gc_last = gc_alls[ci][c - 1:c, lane:lane + 1]
        eg = jnp.exp(gc_col)
        decs.append(jnp.where(incl, jnp.exp(jnp.where(incl, gc_col - gc_row, 0.0)), 0.0))
        rhss.append(jnp.concatenate([beta * vh, (beta * eg) * kh], axis=-1))
        k_ends.append(kh * jnp.exp(gc_last - gc_col))
        g_ends.append(jnp.exp(gc_last))
        qs.append(qh); ks.append(kh); betas.append(beta); egs.append(eg)
    kqs = [_dot_nt(jnp.concatenate([kh, qh], axis=0), kh) for kh, qh in zip(ks, qs)]
    ms = [jnp.where(strict, beta * kq[:c] * dec, 0.0) for beta, kq, dec in zip(betas, kqs, decs)]
    qks = [kq[c:] * dec for kq, dec in zip(kqs, decs)]
    t_invs = _inv_unit_lower(ms, row, col, c)
    sols = [_dot(t_inv, rhs) for t_inv, rhs in zip(t_invs, rhss)]
    q_decs = [qh * eg for qh, eg in zip(qs, egs)]

    s_cur = [s_scr[hd] for hd in range(DN_HEADS)]
    o_chunks = []
    for ci in range(n_chunks):
        idx = [ci * DN_HEADS + hd for hd in range(DN_HEADS)]
        wss = [_dot(jnp.concatenate([sols[i][:, HEAD_DIM:], q_decs[i]], axis=0), s) for i, s in zip(idx, s_cur)]
        us = [sols[i][:, :HEAD_DIM] - ws[:c] for i, ws in zip(idx, wss)]
        outs = [ws[c:] + _dot(qks[i], u) for i, ws, u in zip(idx, wss, us)]
        s_cur = [g_ends[i] * s + _dot_tn(k_ends[i], u) for i, s, u in zip(idx, s_cur, us)]
        o_chunks.append(jnp.concatenate(outs, axis=-1))
        if len(o_chunks) == OUT_CHUNKS or ci == n_chunks - 1:
            rows = slice((ci + 1 - len(o_chunks)) * c, (ci + 1) * c)
            o_dn = _gated_out_norm(jnp.concatenate(o_chunks, axis=0), z_scr[rows, :], onorm_ref[...])
            mix = _dot(jnp.concatenate([o_dn, opool_scr[rows, :]], axis=-1), wout_ref[...])
            y_ref[rows, :] = _layer_norm(DN_ALPHA * h_ref[rows, :] + mix, g2_ref[...], b2_ref[...])
            o_chunks = []
    for hd in range(DN_HEADS):
        s_scr[hd] = s_cur[hd]

    @pl.when(l == n_l - 1)
    def _():
        sfin_ref[...] = s_scr[...]


def _mixer_weight_specs(layer):
    return [
        _layer_resident(layer, (D_MODEL, IN_DIM)),
        _layer_resident(layer, (CONV_W, QKV_DIM)),
        _layer_resident(layer, (1, GATE_PAD)),
        _layer_resident(layer, (1, GATE_PAD)),
        _layer_resident(layer, (1, HEAD_DIM)),
        _layer_resident(layer, (len(POOL_WINDOWS), POOL_GROUP_DIM, POOL_GROUP_DIM)),
        _layer_resident(layer, (1, POOL_DIM)),
        _layer_resident(layer, (DN_DIM + POOL_DIM, D_MODEL)),
        _layer_resident(layer, (1, D_MODEL)),
        _layer_resident(layer, (1, D_MODEL)),
    ]


def _mixer_prompt(layer, h, weights, cast_jobs):
    bsz, seq, _ = h.shape
    tl = MIX_ROWS
    n_l = seq // tl
    assert seq % tl == 0 and tl % CHUNK == 0 and (bsz * n_l) % CAST_STEPS == 0
    n_blocks = bsz * n_l // CAST_STEPS
    out_shape = [
        jax.ShapeDtypeStruct((bsz, seq, D_MODEL), F32),
        jax.ShapeDtypeStruct((bsz, DN_HEADS, HEAD_DIM, HEAD_DIM), F32),
        jax.ShapeDtypeStruct((bsz, CONV_W - 1, QKV_DIM), F32),
        jax.ShapeDtypeStruct((bsz, POOL_BUF, POOL_DIM), F32),
    ]
    out_specs = [
        pl.BlockSpec((None, tl, D_MODEL), lambda b, l: (b, l, 0)),
        pl.BlockSpec((None, DN_HEADS, HEAD_DIM, HEAD_DIM), lambda b, l: (b, 0, 0, 0)),
        pl.BlockSpec((None, CONV_W - 1, QKV_DIM), lambda b, l: (b, 0, 0)),
        pl.BlockSpec((None, POOL_BUF, POOL_DIM), lambda b, l: (b, 0, 0)),
    ]
    cast_specs = []
    for w_all, w_layer in cast_jobs:
        _, n_rows, n_cols = w_all.shape
        assert n_rows % (n_blocks * BF16_SUBLANES) == 0
        blk = n_rows // n_blocks
        cast_specs.append(pl.BlockSpec((None, blk, n_cols),
                                       lambda b, l, w_layer=w_layer: (w_layer, (b * n_l + l) // CAST_STEPS, 0)))
        out_shape.append(jax.ShapeDtypeStruct((n_rows, n_cols), BF16))
        out_specs.append(pl.BlockSpec((blk, n_cols), lambda b, l: ((b * n_l + l) // CAST_STEPS, 0)))
    return pl.pallas_call(
        functools.partial(_mixer_prompt_kernel, len(cast_jobs)),
        out_shape=tuple(out_shape),
        grid=(bsz, n_l),
        in_specs=[
            pl.BlockSpec((None, tl, D_MODEL), lambda b, l: (b, l, 0)),
        ] + _mixer_weight_specs(layer) + cast_specs,
        out_specs=tuple(out_specs),
        scratch_shapes=[
            pltpu.VMEM((DN_HEADS, HEAD_DIM, HEAD_DIM), F32),
            pltpu.VMEM((QKV_DIM // HEAD_DIM, tl + 8, HEAD_DIM), F32),
            pltpu.VMEM((len(POOL_WINDOWS), tl + 16, POOL_GROUP_DIM), F32),
            pltpu.VMEM((QKV_DIM // HEAD_DIM, tl, HEAD_DIM), F32),
            pltpu.VMEM((tl, GATE_PAD), F32),
            pltpu.VMEM((tl, GATE_PAD), F32),
            pltpu.VMEM((tl, DN_DIM), F32),
            pltpu.VMEM((tl, POOL_DIM), F32),
            pltpu.VMEM((D_MODEL, POOL_DIM), BF16),
            pltpu.VMEM((D_MODEL, GATE_PAD), BF16),
        ],
        compiler_params=pltpu.CompilerParams(dimension_semantics=("arbitrary", "arbitrary")),
        name="mixer_prompt",
    )(h, *weights, *[w_all for w_all, _ in cast_jobs])


N_SAMPLE_STATES = 3

def _mixer_sample_kernel(h_ref, s_ref, cbuf_ref, pbuf_ref, win_ref, convw_ref, alog_ref, dtb_ref, onorm_ref,
                         poolw_ref, pscale_ref, wout_ref, g2_ref, b2_ref, *refs):
    refs = refs[N_SAMPLE_STATES:]
    y_ref, snew_ref, cnew_ref, pnew_ref = refs[:4]
    q_scr, k_scr, v_scr, beta_scr, eg_scr, z_scr, opool_scr, o_scr = refs[4:]
    i = pl.program_id(0)
    n_i = pl.num_programs(0)

    @pl.when(i == 0)
    def _():
        proj = jnp.dot(h_ref[...].astype(BF16), win_ref[...], preferred_element_type=F32)
        qkv = proj[:, 0:QKV_DIM]
        conv = convw_ref[CONV_W - 1:CONV_W, :] * qkv
        for j in range(CONV_W - 1):
            conv = conv + convw_ref[j:j + 1, :] * cbuf_ref[j]
        for j in range(CONV_W - 2):
            cnew_ref[j] = cbuf_ref[j + 1]
        cnew_ref[CONV_W - 2] = qkv
        q, k, v = _qkv_heads(conv)
        q_scr[...] = q
        k_scr[...] = k
        v_scr[...] = v
        ba = jnp.concatenate([proj[:, GATE_OFF:GATE_OFF + N_GATE],
                              jnp.zeros((proj.shape[0], GATE_PAD - N_GATE), F32)], axis=-1)
        beta_t, g_t = _gates(ba, alog_ref[...], dtb_ref[...])
        beta_scr[...] = beta_t
        eg_scr[...] = jnp.exp(g_t)
        z_scr[...] = proj[:, Z_OFF:Z_OFF + DN_DIM]

        p = proj[:, POOL_OFF:POOL_OFF + POOL_DIM]
        d_groups = []
        for gi, w in enumerate(POOL_WINDOWS):
            lo = gi * POOL_GROUP_DIM
            tok = p[:, lo:lo + POOL_GROUP_DIM]
            s = tok
            for back in range(1, w):
                s = s + pbuf_ref[POOL_BUF - back, :, lo:lo + POOL_GROUP_DIM]
            cnt = float(min(w, PAST_LEN + 1))
            d_groups.append(s / cnt - tok)
        opool_scr[...] = _pool_project(d_groups, poolw_ref, pscale_ref[...])
        for j in range(POOL_BUF - 1):
            pnew_ref[j] = pbuf_ref[j + 1]
        pnew_ref[POOL_BUF - 1] = p

    rows = pl.ds(pl.multiple_of(i * SAMPLE_TILE, SAMPLE_TILE), SAMPLE_TILE)
    q_t, k_t, v_t = q_scr[rows, :], k_scr[rows, :], v_scr[rows, :]
    beta_t, a_t = beta_scr[rows, :], eg_scr[rows, :]
    eye = jnp.where(lax.broadcasted_iota(jnp.int32, (HEAD_DIM, HEAD_DIM), 0)
                    == lax.broadcasted_iota(jnp.int32, (HEAD_DIM, HEAD_DIM), 1), 1.0, 0.0).astype(F32)
    pad_rows = jnp.zeros((F32_SUBLANES - 2, HEAD_DIM), F32)
    o_heads = [[] for _ in range(SAMPLE_TILE)]
    for hd in range(DN_HEADS):
        sl = slice(hd * HEAD_DIM, (hd + 1) * HEAD_DIM)
        for j in range(SAMPLE_TILE):
            qh = q_t[j:j + 1, sl]
            kh = k_t[j:j + 1, sl]
            vh = v_t[j:j + 1, sl]
            beta = beta_t[j:j + 1, hd:hd + 1]
            a = a_t[j:j + 1, DECAY_LANE + hd:DECAY_LANE + hd + 1]
            s_old = s_ref[j, hd]
            kq = jnp.concatenate([kh, qh, pad_rows], axis=0)
            kq_hi, kq_lo = _split_bf16(kq)
            s_hi, s_lo = _split_bf16(s_old)
            mv = functools.partial(jnp.dot, preferred_element_type=F32)
            kqs = mv(kq_hi, s_hi) + (mv(kq_hi, s_lo) + mv(kq_lo, s_hi))
            ks, qs = kqs[0:1], kqs[1:2]
            k_col = jnp.sum(eye * kh, axis=-1, keepdims=True)
            u = beta * vh - (beta * a) * ks
            qk = jnp.sum(qh * kh, axis=-1, keepdims=True)
            o_heads[j].append(a * qs + qk * u)
            snew_ref[j, hd] = a * s_old + k_col * u
    o_scr[rows, :] = jnp.concatenate([jnp.concatenate(o_h, axis=-1) for o_h in o_heads], axis=0)

    @pl.when(i == n_i - 1)
    def _():
        o_dn = _gated_out_norm(o_scr[...], z_scr[...], onorm_ref[...])
        mix = _dot(jnp.concatenate([o_dn, opool_scr[...]], axis=-1), wout_ref[...])
        y_ref[...] = _layer_norm(DN_ALPHA * h_ref[...] + mix, g2_ref[...], b2_ref[...])


def _mixer_sample(layer, h, s_all, cbuf_all, pbuf_all, new_states, *weights):
    bsz = h.shape[0]
    assert bsz % SAMPLE_TILE == 0 and len(new_states) == N_SAMPLE_STATES
    cshape = (CONV_W - 1, bsz, QKV_DIM)
    pshape = (POOL_BUF, bsz, POOL_DIM)
    assert cbuf_all.shape[1:] == cshape and pbuf_all.shape[1:] == pshape
    out_shape = (
        jax.ShapeDtypeStruct((bsz, D_MODEL), F32),
        jax.ShapeDtypeStruct(s_all.shape, F32),
        jax.ShapeDtypeStruct(cbuf_all.shape, F32),
        jax.ShapeDtypeStruct(pbuf_all.shape, F32),
    )
    state_spec = pl.BlockSpec((None, SAMPLE_TILE, DN_HEADS, HEAD_DIM, HEAD_DIM), lambda i: (layer, i, 0, 0, 0))
    n_in = 4 + 10
    return pl.pallas_call(
        _mixer_sample_kernel,
        out_shape=out_shape,
        grid=(bsz // SAMPLE_TILE,),
        in_specs=[
            _resident((bsz, D_MODEL)),
            state_spec,
            _layer_resident(layer, cshape),
            _layer_resident(layer, pshape),
        ] + _mixer_weight_specs(layer) + [pl.BlockSpec(memory_space=pl.ANY)] * len(new_states),
        out_specs=(
            pl.BlockSpec((bsz, D_MODEL), lambda i: (0, 0)),
            state_spec,
            pl.BlockSpec((None,) + cshape, lambda i: (layer, 0, 0, 0)),
            pl.BlockSpec((None,) + pshape, lambda i: (layer, 0, 0, 0)),
        ),
        input_output_aliases={n_in + k: 1 + k for k in range(len(new_states))},
        scratch_shapes=[
            pltpu.VMEM((bsz, DN_DIM), F32),
            pltpu.VMEM((bsz, DN_DIM), F32),
            pltpu.VMEM((bsz, DN_DIM), F32),
            pltpu.VMEM((bsz, GATE_PAD), F32),
            pltpu.VMEM((bsz, GATE_PAD), F32),
            pltpu.VMEM((bsz, DN_DIM), F32),
            pltpu.VMEM((bsz, POOL_DIM), F32),
            pltpu.VMEM((bsz, DN_DIM), F32),
        ],
        compiler_params=pltpu.CompilerParams(dimension_semantics=("arbitrary",)),
        name="mixer_sample",
    )(h, s_all, cbuf_all, pbuf_all, *weights, *new_states)


def _pad_gate_param(v):
    out = jnp.zeros((v.shape[0], 1, GATE_PAD), F32)
    return out.at[:, 0, DECAY_LANE:DECAY_LANE + DN_HEADS].set(v.astype(F32))


def kernel(x_prompt, x_sample, state_delta, state_conv, state_pool, ln1_g, ln1_b, ffn1_w_gate, ffn1_w_up, ffn1_w_down, w_in, conv_w, a_log, dt_bias, onorm_g, pool_w, pool_scale, w_out, ln2_g, ln2_b, ffn2_w_gate, ffn2_w_up, ffn2_w_down, ln3_g, ln3_b):
    bsz, seq, _ = x_prompt.shape
    dec_b = x_sample.shape[0]

    win = w_in.astype(BF16)
    wout = w_out.astype(BF16)
    poolw = pool_w.astype(BF16)
    ffn1_stacks = (ffn1_w_gate, ffn1_w_up, ffn1_w_down)
    ffn2_stacks = (ffn2_w_gate, ffn2_w_up, ffn2_w_down)
    f1 = tuple(w[0].astype(BF16) for w in ffn1_stacks)
    alog = _pad_gate_param(a_log)
    dtb = _pad_gate_param(dt_bias)
    row = lambda v: v.reshape(DEPTH, 1, -1)
    ln1g, ln1b, ln2g, ln2b, ln3g, ln3b = map(row, (ln1_g, ln1_b, ln2_g, ln2_b, ln3_g, ln3_b))
    onorm = row(onorm_g)
    pscale = row(pool_scale)
    mix_w = (win, conv_w, alog, dtb, onorm, poolw, pscale, wout, ln2g, ln2b)

    xp = x_prompt.reshape(bsz * seq, D_MODEL)
    xs = x_sample.reshape(dec_b, D_MODEL)
    dp, cp, pp = [], [], []
    cbuf_all = jnp.transpose(state_conv, (0, 2, 1, 3))
    pbuf_all = jnp.transpose(state_pool, (0, 2, 1, 3))
    sample_states = tuple(jnp.zeros(s.shape, F32) for s in (state_delta, cbuf_all, pbuf_all))
    for l in range(DEPTH):
        hp, hs = _ffn_ln(l, xp, xs, *f1, ln1g, ln1b)
        cast_jobs = [(w, l) for w in ffn2_stacks]
        if l + 1 < DEPTH:
            cast_jobs += [(w, l + 1) for w in ffn1_stacks]
        hp, s_p, c_p, p_p, *cast = _mixer_prompt(l, hp.reshape(bsz, seq, D_MODEL), mix_w, cast_jobs)
        f2, f1 = tuple(cast[:3]), tuple(cast[3:])
        hs, *sample_states = _mixer_sample(l, hs, state_delta, cbuf_all, pbuf_all, tuple(sample_states), *mix_w)
        xp, xs = _ffn_ln(l, hp.reshape(bsz * seq, D_MODEL), hs, *f2, ln3g, ln3b)
        dp.append(s_p); cp.append(c_p); pp.append(p_p)
    delta_sample, conv_sample, pool_sample = sample_states
    return (xp.reshape(bsz, seq, D_MODEL), xs.reshape(dec_b, 1, D_MODEL),
            jnp.stack(dp), jnp.stack(cp), jnp.stack(pp),
            delta_sample, jnp.transpose(conv_sample, (0, 2, 1, 3)), jnp.transpose(pool_sample, (0, 2, 1, 3)))
```

```python
import functools

import jax
import jax.numpy as jnp
from jax import lax
from jax.experimental import pallas as pl
from jax.experimental.pallas import tpu as pltpu

F32 = jnp.float32
BF16 = jnp.bfloat16

D_MODEL = 1024
DEPTH = 4
DN_HEADS = 4
HEAD_DIM = 128
DN_DIM = DN_HEADS * HEAD_DIM
POOL_DIM = 512
QKV_DIM = 3 * DN_DIM
CONV_W = 4
POOL_WINDOWS = (2, 4, 8, 16)
POOL_GROUP_DIM = POOL_DIM // len(POOL_WINDOWS)
POOL_BUF = max(POOL_WINDOWS) - 1
D_FF = 2816
PAST_LEN = 16384
DN_ALPHA = (2.0 * DEPTH) ** 0.25
LN_EPS = 1e-5
RMS_EPS = 1e-6
L2_EPS = 1e-6

N_GATE = 2 * DN_HEADS
Z_OFF = QKV_DIM
GATE_OFF = QKV_DIM + DN_DIM
POOL_OFF = GATE_OFF + N_GATE
IN_DIM = POOL_OFF + POOL_DIM
GATE_PAD = 128
DECAY_LANE = DN_HEADS

CHUNK = 128
INV_BASE = 8
FFN_ROWS = 1024
FFN_SUB_ROWS = 256
MIX_ROWS = 512
OUT_CHUNKS = 2
CAST_STEPS = 2
BF16_SUBLANES = 16
F32_SUBLANES = 8
SAMPLE_TILE = 8


def _dot(a, b):
    return jnp.dot(a.astype(BF16), b.astype(BF16), preferred_element_type=F32)


def _dot_nt(a, b):
    return lax.dot_general(a.astype(BF16), b.astype(BF16), (((1,), (1,)), ((), ())),
                           preferred_element_type=F32)


def _dot_tn(a, b):
    return lax.dot_general(a.astype(BF16), b.astype(BF16), (((0,), (0,)), ((), ())),
                           preferred_element_type=F32)


def _split_bf16(x):
    hi = x.astype(BF16)
    lo = (x - hi.astype(F32)).astype(BF16)
    return hi, lo


def _dot_exact_lhs(a_bf16, b):
    b1 = b.astype(BF16)
    r1 = b - b1.astype(F32)
    b2 = r1.astype(BF16)
    b3 = (r1 - b2.astype(F32)).astype(BF16)
    d = functools.partial(jnp.dot, preferred_element_type=F32)
    return d(a_bf16, b1) + (d(a_bf16, b2) + d(a_bf16, b3))


def _silu(x):
    return x * jax.nn.sigmoid(x)


def _softplus(x):
    return jnp.maximum(x, 0.0) + jnp.log1p(jnp.exp(-jnp.abs(x)))


def _layer_norm(y, g, b):
    mu = jnp.mean(y, axis=-1, keepdims=True)
    yc = y - mu
    var = jnp.mean(yc * yc, axis=-1, keepdims=True)
    return yc * lax.rsqrt(var + LN_EPS) * g + b


def _l2norm(x):
    return x * lax.rsqrt(jnp.sum(x * x, axis=-1, keepdims=True) + L2_EPS)


def _ffn_ln_rows(x, wg_ref, wu_ref, wd_ref, g_ref, b_ref):
    xb = x.astype(BF16)
    gate = jnp.dot(xb, wg_ref[...], preferred_element_type=F32)
    up = jnp.dot(xb, wu_ref[...], preferred_element_type=F32)
    act = (_silu(gate) * up).astype(BF16)
    ff = jnp.dot(act, wd_ref[...], preferred_element_type=F32)
    return _layer_norm(DN_ALPHA * x + 0.5 * ff, g_ref[...], b_ref[...])


def _ffn_ln_kernel(x_ref, xs_ref, wg_ref, wu_ref, wd_ref, g_ref, b_ref, o_ref, os_ref):
    weights = (wg_ref, wu_ref, wd_ref, g_ref, b_ref)
    tm = x_ref.shape[0]
    for r in range(tm // FFN_SUB_ROWS):
        rows = slice(r * FFN_SUB_ROWS, (r + 1) * FFN_SUB_ROWS)
        o_ref[rows, :] = _ffn_ln_rows(x_ref[rows, :], *weights)

    @pl.when(pl.program_id(0) == pl.num_programs(0) - 1)
    def _():
        os_ref[...] = _ffn_ln_rows(xs_ref[...], *weights)


def _resident(shape):
    zeros = (0,) * len(shape)
    return pl.BlockSpec(shape, lambda *_: zeros, pipeline_mode=pl.Buffered(1))


def _layer_resident(layer, shape):
    index = (layer,) + (0,) * len(shape)
    return pl.BlockSpec((None,) + tuple(shape), lambda *_: index, pipeline_mode=pl.Buffered(1))


def _ffn_ln(layer, x, xs, wg, wu, wd, g, b):
    rows = x.shape[0]
    tm = FFN_ROWS
    assert rows % tm == 0 and tm % FFN_SUB_ROWS == 0
    return pl.pallas_call(
        _ffn_ln_kernel,
        out_shape=(jax.ShapeDtypeStruct((rows, D_MODEL), F32), jax.ShapeDtypeStruct(xs.shape, F32)),
        grid=(rows // tm,),
        in_specs=[
            pl.BlockSpec((tm, D_MODEL), lambda i: (i, 0)),
            _resident(xs.shape),
            _resident((D_MODEL, D_FF)),
            _resident((D_MODEL, D_FF)),
            _resident((D_FF, D_MODEL)),
            _layer_resident(layer, (1, D_MODEL)),
            _layer_resident(layer, (1, D_MODEL)),
        ],
        out_specs=(pl.BlockSpec((tm, D_MODEL), lambda i: (i, 0)),
                   pl.BlockSpec(xs.shape, lambda i: (0, 0))),
        compiler_params=pltpu.CompilerParams(dimension_semantics=("arbitrary",)),
        name="ffn_ln",
    )(x, xs, wg, wu, wd, g, b)


def _gates(ba, alog, dtb):
    beta = jax.nn.sigmoid(ba)
    g = -jnp.exp(alog) * _softplus(ba + dtb)
    return beta, g


def _qkv_heads(conv_out):
    act = _silu(conv_out)
    qs, ks = [], []
    for h in range(DN_HEADS):
        lo = h * HEAD_DIM
        qs.append(_l2norm(act[:, lo:lo + HEAD_DIM]) * (HEAD_DIM ** -0.5))
        ks.append(_l2norm(act[:, DN_DIM + lo:DN_DIM + lo + HEAD_DIM]))
    return jnp.concatenate(qs, -1), jnp.concatenate(ks, -1), act[:, 2 * DN_DIM:]


def _gated_out_norm(o, z, onorm_g):
    outs = []
    for h in range(DN_HEADS):
        sl = slice(h * HEAD_DIM, (h + 1) * HEAD_DIM)
        oh = o[:, sl]
        oh = oh * lax.rsqrt(jnp.mean(oh * oh, axis=-1, keepdims=True) + RMS_EPS)
        outs.append(oh * onorm_g * _silu(z[:, sl]))
    return jnp.concatenate(outs, -1)


def _pool_project(d_groups, poolw_ref, pscale):
    ys = [_dot(d, poolw_ref[gi]) for gi, d in enumerate(d_groups)]
    return jnp.concatenate(ys, -1) * pscale


def _inv_unit_lower(ms, row, col, size):
    eye = jnp.where(row == col, 1.0, 0.0).astype(F32)
    base_bits = INV_BASE.bit_length() - 1
    base_mask = (row >> base_bits) == (col >> base_bits)
    ps = [jnp.where(base_mask, m, 0.0) for m in ms]
    xs = [eye - p for p in ps]
    ps = [_dot(p, p) for p in ps]
    for step in range(base_bits - 1):
        if step < base_bits - 2:
            xps = [_dot(jnp.concatenate([x, p], axis=0), p) for x, p in zip(xs, ps)]
            xs = [x + xp[:size] for x, xp in zip(xs, xps)]
            ps = [xp[size:] for xp in xps]
        else:
            xs = [x + _dot(x, p) for x, p in zip(xs, ps)]
    bits = base_bits
    while (1 << bits) < size:
        same_pair = (row >> (bits + 1)) == (col >> (bits + 1))
        lower_left = (((row >> bits) & 1) == 1) & (((col >> bits) & 1) == 0)
        mask = same_pair & lower_left
        ys = [_dot(jnp.where(mask, m, 0.0), x) for m, x in zip(ms, xs)]
        xs = [x - _dot(x, y) for x, y in zip(xs, ys)]
        bits += 1
    return xs


def _mixer_prompt_kernel(n_cast, n_zero, h_ref, win_ref, convw_ref, alog_ref, dtb_ref, onorm_ref, poolw_ref,
                         pscale_ref, wout_ref, g2_ref, b2_ref, *refs):
    cast_src = refs[:n_cast]
    y_ref, sfin_ref, cfin_ref, pfin_ref = refs[n_cast:n_cast + 4]
    cast_dst = refs[n_cast + 4:2 * n_cast + 4]
    zero_dst = refs[2 * n_cast + 4:2 * n_cast + 4 + n_zero]
    s_scr, ext_scr, pext_scr, qkv_scr, beta_scr, g_scr, z_scr, opool_scr, pw_scr, gw_scr = (
        refs[2 * n_cast + 4 + n_zero:])
    tl = h_ref.shape[0]
    l = pl.program_id(1)
    n_l = pl.num_programs(1)

    @pl.when((pl.program_id(0) * n_l + l) % CAST_STEPS == 0)
    def _():
        for src, dst in zip(cast_src, cast_dst):
            dst[...] = src[...].astype(BF16)

    for dst in zero_dst:
        dst[...] = jnp.zeros(dst.shape, dst.dtype)

    @pl.when(l == 0)
    def _():
        s_scr[...] = jnp.zeros_like(s_scr)
        ext_scr[:, 0:8, :] = jnp.zeros((QKV_DIM // HEAD_DIM, 8, HEAD_DIM), F32)
        pext_scr[:, 0:16, :] = jnp.zeros((len(POOL_WINDOWS), 16, POOL_GROUP_DIM), F32)

    hb = h_ref[...].astype(BF16)

    @pl.when((pl.program_id(0) == 0) & (l == 0))
    def _():
        pw_scr[...] = win_ref[:, POOL_OFF:POOL_OFF + POOL_DIM]
        gw_scr[...] = jnp.concatenate([win_ref[:, GATE_OFF:GATE_OFF + N_GATE],
                                       jnp.zeros((D_MODEL, GATE_PAD - N_GATE), BF16)], axis=-1)

    def project(w_ref, lo, width):
        return jnp.dot(hb, w_ref[:, lo:lo + width], preferred_element_type=F32)

    def conv_slab(t, x_t):
        lanes = slice(t * HEAD_DIM, (t + 1) * HEAD_DIM)
        ext_scr[t, 8:8 + tl, :] = x_t
        conv = convw_ref[CONV_W - 1:CONV_W, lanes] * x_t
        for i in range(CONV_W - 1):
            conv = conv + convw_ref[i:i + 1, lanes] * ext_scr[t, 5 + i:5 + i + tl, :]
        cfin_ref[:, lanes] = ext_scr[t, tl + 5:tl + 8, :]
        ext_scr[t, 0:8, :] = ext_scr[t, tl:tl + 8, :]
        act = _silu(conv)
        if t < DN_HEADS:
            act = _l2norm(act) * (HEAD_DIM ** -0.5)
        elif t < 2 * DN_HEADS:
            act = _l2norm(act)
        qkv_scr[t] = act

    def pool_slab(gi, tok):
        w = POOL_WINDOWS[gi]
        lanes = slice(gi * POOL_GROUP_DIM, (gi + 1) * POOL_GROUP_DIM)
        pext_scr[gi, 16:16 + tl, :] = tok
        s = tok
        for i in range(1, w):
            s = s + pext_scr[gi, 16 - i:16 - i + tl, :]
        pos = l * tl + lax.broadcasted_iota(jnp.int32, (tl, 1), 0)
        cnt = jnp.minimum(w, pos + 1).astype(F32)
        pfin_ref[:, lanes] = pext_scr[gi, tl + 1:tl + 16, :]
        pext_scr[gi, 0:16, :] = pext_scr[gi, tl:tl + 16, :]
        return s / cnt - tok

    group = 2 * HEAD_DIM
    for j in range(QKV_DIM // group):
        y = project(win_ref, j * group, group)
        conv_slab(2 * j, y[:, :HEAD_DIM])
        conv_slab(2 * j + 1, y[:, HEAD_DIM:])
    beta_t, g_t = _gates(project(gw_scr, 0, GATE_PAD), alog_ref[...], dtb_ref[...])
    beta_scr[...] = beta_t
    g_scr[...] = g_t
    d_groups = []
    for j in range(POOL_DIM // group):
        y = project(pw_scr, j * group, group)
        d_groups.append(pool_slab(2 * j, y[:, :POOL_GROUP_DIM]))
        d_groups.append(pool_slab(2 * j + 1, y[:, POOL_GROUP_DIM:]))
    opool_scr[...] = _pool_project(d_groups, poolw_ref, pscale_ref[...])
    z_scr[...] = project(win_ref, Z_OFF, DN_DIM)

    c = CHUNK
    row = lax.broadcasted_iota(jnp.int32, (c, c), 0)
    col = lax.broadcasted_iota(jnp.int32, (c, c), 1)
    incl = row >= col
    strict = row > col
    tri = jnp.where(incl, 1.0, 0.0).astype(BF16)

    n_chunks = tl // c
    chains = [(ci, hd) for ci in range(n_chunks) for hd in range(DN_HEADS)]
    gc_alls, gc_ts, beta_alls = [], [], []
    for ci in range(n_chunks):
        rows = slice(ci * c, (ci + 1) * c)
        gc_all = _dot_exact_lhs(tri, g_scr[rows, :])
        gc_alls.append(gc_all)
        gc_ts.append(gc_all.T)
        beta_alls.append(beta_scr[rows, :])
    qs, ks, betas, egs, decs, rhss, k_ends, g_ends = [], [], [], [], [], [], [], []
    for ci, hd in chains:
        rows = slice(ci * c, (ci + 1) * c)
        sl = slice(hd * HEAD_DIM, (hd + 1) * HEAD_DIM)
        lane = DECAY_LANE + hd
        qh, kh, vh = qkv_scr[hd, rows, :], qkv_scr[DN_HEADS + hd, rows, :], qkv_scr[2 * DN_HEADS + hd, rows, :]
        beta = beta_alls[ci][:, hd:hd + 1]
        gc_col = gc_alls[ci][:, lane:lane + 1]
        gc_row = gc_ts[ci][lane:lane + 1, :]
        gc_last = gc_alls[ci][c - 1:c, lane:lane + 1]
        eg = jnp.exp(gc_col)
        decs.append(jnp.where(incl, jnp.exp(jnp.where(incl, gc_col - gc_row, 0.0)), 0.0))
        rhss.append(jnp.concatenate([beta * vh, (beta * eg) * kh], axis=-1))
        k_ends.append(kh * jnp.exp(gc_last - gc_col))
        g_ends.append(jnp.exp(gc_last))
        qs.append(qh); ks.append(kh); betas.append(beta); egs.append(eg)
    kqs = [_dot_nt(jnp.concatenate([kh, qh], axis=0), kh) for kh, qh in zip(ks, qs)]
    ms = [jnp.where(strict, beta * kq[:c] * dec, 0.0) for beta, kq, dec in zip(betas, kqs, decs)]
    qks = [kq[c:] * dec for kq, dec in zip(kqs, decs)]
    t_invs = _inv_unit_lower(ms, row, col, c)
    sols = [_dot(t_inv, rhs) for t_inv, rhs in zip(t_invs, rhss)]
    q_decs = [qh * eg for qh, eg in zip(qs, egs)]

    s_cur = [s_scr[hd] for hd in range(DN_HEADS)]
    o_chunks = []
    for ci in range(n_chunks):
        idx = [ci * DN_HEADS + hd for hd in range(DN_HEADS)]
        wss = [_dot(jnp.concatenate([sols[i][:, HEAD_DIM:], q_decs[i]], axis=0), s) for i, s in zip(idx, s_cur)]
        us = [sols[i][:, :HEAD_DIM] - ws[:c] for i, ws in zip(idx, wss)]
        outs = [ws[c:] + _dot(qks[i], u) for i, ws, u in zip(idx, wss, us)]
        s_cur = [g_ends[i] * s + _dot_tn(k_ends[i], u) for i, s, u in zip(idx, s_cur, us)]
        o_chunks.append(jnp.concatenate(outs, axis=-1))
        if len(o_chunks) == OUT_CHUNKS or ci == n_chunks - 1:
            rows = slice((ci + 1 - len(o_chunks)) * c, (ci + 1) * c)
            o_dn = _gated_out_norm(jnp.concatenate(o_chunks, axis=0), z_scr[rows, :], onorm_ref[...])
            mix = _dot(jnp.concatenate([o_dn, opool_scr[rows, :]], axis=-1), wout_ref[...])
            y_ref[rows, :] = _layer_norm(DN_ALPHA * h_ref[rows, :] + mix, g2_ref[...], b2_ref[...])
            o_chunks = []
    for hd in range(DN_HEADS):
        s_scr[hd] = s_cur[hd]

    @pl.when(l == n_l - 1)
    def _():
        sfin_ref[...] = s_scr[...]


def _mixer_weight_specs(layer):
    return [
        _layer_resident(layer, (D_MODEL, IN_DIM)),
        _layer_resident(layer, (CONV_W, QKV_DIM)),
        _layer_resident(layer, (1, GATE_PAD)),
        _layer_resident(layer, (1, GATE_PAD)),
        _layer_resident(layer, (1, HEAD_DIM)),
        _layer_resident(layer, (len(POOL_WINDOWS), POOL_GROUP_DIM, POOL_GROUP_DIM)),
        _layer_resident(layer, (1, POOL_DIM)),
        _layer_resident(layer, (DN_DIM + POOL_DIM, D_MODEL)),
        _layer_resident(layer, (1, D_MODEL)),
        _layer_resident(layer, (1, D_MODEL)),
    ]


def _mixer_prompt(layer, h, weights, cast_jobs, zero_shapes=()):
    bsz, seq, _ = h.shape
    tl = MIX_ROWS
    n_l = seq // tl
    assert seq % tl == 0 and tl % CHUNK == 0 and (bsz * n_l) % CAST_STEPS == 0
    n_blocks = bsz * n_l // CAST_STEPS
    out_shape = [
        jax.ShapeDtypeStruct((bsz, seq, D_MODEL), F32),
        jax.ShapeDtypeStruct((bsz, DN_HEADS, HEAD_DIM, HEAD_DIM), F32),
        jax.ShapeDtypeStruct((bsz, CONV_W - 1, QKV_DIM), F32),
        jax.ShapeDtypeStruct((bsz, POOL_BUF, POOL_DIM), F32),
    ]
    out_specs = [
        pl.BlockSpec((None, tl, D_MODEL), lambda b, l: (b, l, 0)),
        pl.BlockSpec((None, DN_HEADS, HEAD_DIM, HEAD_DIM), lambda b, l: (b, 0, 0, 0)),
        pl.BlockSpec((None, CONV_W - 1, QKV_DIM), lambda b, l: (b, 0, 0)),
        pl.BlockSpec((None, POOL_BUF, POOL_DIM), lambda b, l: (b, 0, 0)),
    ]
    cast_specs = []
    for w_all, w_layer in cast_jobs:
        _, n_rows, n_cols = w_all.shape
        assert n_rows % (n_blocks * BF16_SUBLANES) == 0
        blk = n_rows // n_blocks
        cast_specs.append(pl.BlockSpec((None, blk, n_cols),
                                       lambda b, l, w_layer=w_layer: (w_layer, (b * n_l + l) // CAST_STEPS, 0)))
        out_shape.append(jax.ShapeDtypeStruct((n_rows, n_cols), BF16))
        out_specs.append(pl.BlockSpec((blk, n_cols), lambda b, l: ((b * n_l + l) // CAST_STEPS, 0)))
    for shape in zero_shapes:
        assert shape[0] % (bsz * n_l) == 0
        rest = tuple(shape[1:])
        out_shape.append(jax.ShapeDtypeStruct(tuple(shape), F32))
        out_specs.append(pl.BlockSpec((shape[0] // (bsz * n_l),) + rest,
                                      lambda b, l, nd=len(rest): (b * n_l + l,) + (0,) * nd))
    return pl.pallas_call(
        functools.partial(_mixer_prompt_kernel, len(cast_jobs), len(zero_shapes)),
        out_shape=tuple(out_shape),
        grid=(bsz, n_l),
        in_specs=[
            pl.BlockSpec((None, tl, D_MODEL), lambda b, l: (b, l, 0)),
        ] + _mixer_weight_specs(layer) + cast_specs,
        out_specs=tuple(out_specs),
        scratch_shapes=[
            pltpu.VMEM((DN_HEADS, HEAD_DIM, HEAD_DIM), F32),
            pltpu.VMEM((QKV_DIM // HEAD_DIM, tl + 8, HEAD_DIM), F32),
            pltpu.VMEM((len(POOL_WINDOWS), tl + 16, POOL_GROUP_DIM), F32),
            pltpu.VMEM((QKV_DIM // HEAD_DIM, tl, HEAD_DIM), F32),
            pltpu.VMEM((tl, GATE_PAD), F32),
            pltpu.VMEM((tl, GATE_PAD), F32),
            pltpu.VMEM((tl, DN_DIM), F32),
            pltpu.VMEM((tl, POOL_DIM), F32),
            pltpu.VMEM((D_MODEL, POOL_DIM), BF16),
            pltpu.VMEM((D_MODEL, GATE_PAD), BF16),
        ],
        compiler_params=pltpu.CompilerParams(dimension_semantics=("arbitrary", "arbitrary")),
        name="mixer_prompt",
    )(h, *weights, *[w_all for w_all, _ in cast_jobs])


N_SAMPLE_STATES = 3

def _mixer_sample_kernel(h_ref, s_ref, cbuf_ref, pbuf_ref, win_ref, convw_ref, alog_ref, dtb_ref, onorm_ref,
                         poolw_ref, pscale_ref, wout_ref, g2_ref, b2_ref, *refs):
    refs = refs[N_SAMPLE_STATES:]
    y_ref, snew_ref, cnew_ref, pnew_ref = refs[:4]
    q_scr, k_scr, v_scr, beta_scr, eg_scr, z_scr, opool_scr, o_scr = refs[4:]
    i = pl.program_id(0)
    n_i = pl.num_programs(0)

    @pl.when(i == 0)
    def _():
        proj = jnp.dot(h_ref[...].astype(BF16), win_ref[...], preferred_element_type=F32)
        qkv = proj[:, 0:QKV_DIM]
        conv = convw_ref[CONV_W - 1:CONV_W, :] * qkv
        for j in range(CONV_W - 1):
            conv = conv + convw_ref[j:j + 1, :] * cbuf_ref[j]
        for j in range(CONV_W - 2):
            cnew_ref[j] = cbuf_ref[j + 1]
        cnew_ref[CONV_W - 2] = qkv
        q, k, v = _qkv_heads(conv)
        q_scr[...] = q
        k_scr[...] = k
        v_scr[...] = v
        ba = jnp.concatenate([proj[:, GATE_OFF:GATE_OFF + N_GATE],
                              jnp.zeros((proj.shape[0], GATE_PAD - N_GATE), F32)], axis=-1)
        beta_t, g_t = _gates(ba, alog_ref[...], dtb_ref[...])
        beta_scr[...] = beta_t
        eg_scr[...] = jnp.exp(g_t)
        z_scr[...] = proj[:, Z_OFF:Z_OFF + DN_DIM]

        p = proj[:, POOL_OFF:POOL_OFF + POOL_DIM]
        d_groups = []
        for gi, w in enumerate(POOL_WINDOWS):
            lo = gi * POOL_GROUP_DIM
            tok = p[:, lo:lo + POOL_GROUP_DIM]
            s = tok
            for back in range(1, w):
                s = s + pbuf_ref[POOL_BUF - back, :, lo:lo + POOL_GROUP_DIM]
            cnt = float(min(w, PAST_LEN + 1))
            d_groups.append(s / cnt - tok)
        opool_scr[...] = _pool_project(d_groups, poolw_ref, pscale_ref[...])
        for j in range(POOL_BUF - 1):
            pnew_ref[j] = pbuf_ref[j + 1]
        pnew_ref[POOL_BUF - 1] = p

    rows = pl.ds(pl.multiple_of(i * SAMPLE_TILE, SAMPLE_TILE), SAMPLE_TILE)
    q_t, k_t, v_t = q_scr[rows, :], k_scr[rows, :], v_scr[rows, :]
    beta_t, a_t = beta_scr[rows, :], eg_scr[rows, :]
    eye = jnp.where(lax.broadcasted_iota(jnp.int32, (HEAD_DIM, HEAD_DIM), 0)
                    == lax.broadcasted_iota(jnp.int32, (HEAD_DIM, HEAD_DIM), 1), 1.0, 0.0).astype(F32)
    pad_rows = jnp.zeros((F32_SUBLANES - 2, HEAD_DIM), F32)
    o_heads = [[] for _ in range(SAMPLE_TILE)]
    for hd in range(DN_HEADS):
        sl = slice(hd * HEAD_DIM, (hd + 1) * HEAD_DIM)
        for j in range(SAMPLE_TILE):
            qh = q_t[j:j + 1, sl]
            kh = k_t[j:j + 1, sl]
            vh = v_t[j:j + 1, sl]
            beta = beta_t[j:j + 1, hd:hd + 1]
            a = a_t[j:j + 1, DECAY_LANE + hd:DECAY_LANE + hd + 1]
            s_old = s_ref[j, hd]
            kq = jnp.concatenate([kh, qh, pad_rows], axis=0)
            kq_hi, kq_lo = _split_bf16(kq)
            s_hi, s_lo = _split_bf16(s_old)
            mv = functools.partial(jnp.dot, preferred_element_type=F32)
            kqs = mv(kq_hi, s_hi) + (mv(kq_hi, s_lo) + mv(kq_lo, s_hi))
            ks, qs = kqs[0:1], kqs[1:2]
            k_col = jnp.sum(eye * kh, axis=-1, keepdims=True)
            u = beta * vh - (beta * a) * ks
            qk = jnp.sum(qh * kh, axis=-1, keepdims=True)
            o_heads[j].append(a * qs + qk * u)
            snew_ref[j, hd] = a * s_old + k_col * u
    o_scr[rows, :] = jnp.concatenate([jnp.concatenate(o_h, axis=-1) for o_h in o_heads], axis=0)

    @pl.when(i == n_i - 1)
    def _():
        o_dn = _gated_out_norm(o_scr[...], z_scr[...], onorm_ref[...])
        mix = _dot(jnp.concatenate([o_dn, opool_scr[...]], axis=-1), wout_ref[...])
        y_ref[...] = _layer_norm(DN_ALPHA * h_ref[...] + mix, g2_ref[...], b2_ref[...])


def _mixer_sample(layer, h, s_all, cbuf_all, pbuf_all, new_states, *weights):
    bsz = h.shape[0]
    assert bsz % SAMPLE_TILE == 0 and len(new_states) == N_SAMPLE_STATES
    cshape = (CONV_W - 1, bsz, QKV_DIM)
    pshape = (POOL_BUF, bsz, POOL_DIM)
    assert cbuf_all.shape[1:] == cshape and pbuf_all.shape[1:] == pshape
    out_shape = (
        jax.ShapeDtypeStruct((bsz, D_MODEL), F32),
        jax.ShapeDtypeStruct(s_all.shape, F32),
        jax.ShapeDtypeStruct(cbuf_all.shape, F32),
        jax.ShapeDtypeStruct(pbuf_all.shape, F32),
    )
    state_spec = pl.BlockSpec((None, SAMPLE_TILE, DN_HEADS, HEAD_DIM, HEAD_DIM), lambda i: (layer, i, 0, 0, 0))
    n_in = 4 + 10
    return pl.pallas_call(
        _mixer_sample_kernel,
        out_shape=out_shape,
        grid=(bsz // SAMPLE_TILE,),
        in_specs=[
            _resident((bsz, D_MODEL)),
            state_spec,
            _layer_resident(layer, cshape),
            _layer_resident(layer, pshape),
        ] + _mixer_weight_specs(layer) + [pl.BlockSpec(memory_space=pl.ANY)] * len(new_states),
        out_specs=(
            pl.BlockSpec((bsz, D_MODEL), lambda i: (0, 0)),
            state_spec,
            pl.BlockSpec((None,) + cshape, lambda i: (layer, 0, 0, 0)),
            pl.BlockSpec((None,) + pshape, lambda i: (layer, 0, 0, 0)),
        ),
        input_output_aliases={n_in + k: 1 + k for k in range(len(new_states))},
        scratch_shapes=[
            pltpu.VMEM((bsz, DN_DIM), F32),
            pltpu.VMEM((bsz, DN_DIM), F32),
            pltpu.VMEM((bsz, DN_DIM), F32),
            pltpu.VMEM((bsz, GATE_PAD), F32),
            pltpu.VMEM((bsz, GATE_PAD), F32),
            pltpu.VMEM((bsz, DN_DIM), F32),
            pltpu.VMEM((bsz, POOL_DIM), F32),
            pltpu.VMEM((bsz, DN_DIM), F32),
        ],
        compiler_params=pltpu.CompilerParams(dimension_semantics=("arbitrary",)),
        name="mixer_sample",
    )(h, s_all, cbuf_all, pbuf_all, *weights, *new_states)


def _pad_gate_param(v):
    out = jnp.zeros((v.shape[0], 1, GATE_PAD), F32)
    return out.at[:, 0, DECAY_LANE:DECAY_LANE + DN_HEADS].set(v.astype(F32))


def kernel(x_prompt, x_sample, state_delta, state_conv, state_pool, ln1_g, ln1_b, ffn1_w_gate, ffn1_w_up, ffn1_w_down, w_in, conv_w, a_log, dt_bias, onorm_g, pool_w, pool_scale, w_out, ln2_g, ln2_b, ffn2_w_gate, ffn2_w_up, ffn2_w_down, ln3_g, ln3_b):
    bsz, seq, _ = x_prompt.shape
    dec_b = x_sample.shape[0]

    win = w_in.astype(BF16)
    wout = w_out.astype(BF16)
    poolw = pool_w.astype(BF16)
    ffn1_stacks = (ffn1_w_gate, ffn1_w_up, ffn1_w_down)
    ffn2_stacks = (ffn2_w_gate, ffn2_w_up, ffn2_w_down)
    f1 = tuple(w[0].astype(BF16) for w in ffn1_stacks)
    alog = _pad_gate_param(a_log)
    dtb = _pad_gate_param(dt_bias)
    row = lambda v: v.reshape(DEPTH, 1, -1)
    ln1g, ln1b, ln2g, ln2b, ln3g, ln3b = map(row, (ln1_g, ln1_b, ln2_g, ln2_b, ln3_g, ln3_b))
    onorm = row(onorm_g)
    pscale = row(pool_scale)
    mix_w = (win, conv_w, alog, dtb, onorm, poolw, pscale, wout, ln2g, ln2b)

    xp = x_prompt.reshape(bsz * seq, D_MODEL)
    xs = x_sample.reshape(dec_b, D_MODEL)
    dp, cp, pp = [], [], []
    cbuf_all = jnp.transpose(state_conv, (0, 2, 1, 3))
    pbuf_all = jnp.transpose(state_pool, (0, 2, 1, 3))
    state_shapes = [state_delta.shape, cbuf_all.shape, pbuf_all.shape]
    flat_shapes = [(DEPTH * dec_b,) + state_delta.shape[2:],
                   (DEPTH * (CONV_W - 1) * dec_b, QKV_DIM), (DEPTH * POOL_BUF * dec_b, POOL_DIM)]
    for l in range(DEPTH):
        hp, hs = _ffn_ln(l, xp, xs, *f1, ln1g, ln1b)
        cast_jobs = [(w, l) for w in ffn2_stacks]
        if l + 1 < DEPTH:
            cast_jobs += [(w, l + 1) for w in ffn1_stacks]
        hp, s_p, c_p, p_p, *extra = _mixer_prompt(l, hp.reshape(bsz, seq, D_MODEL), mix_w, cast_jobs,
                                                  flat_shapes if l == 0 else ())
        cast = extra[:len(cast_jobs)]
        if l == 0:
            sample_states = tuple(z.reshape(s) for z, s in zip(extra[len(cast_jobs):], state_shapes))
        f2, f1 = tuple(cast[:3]), tuple(cast[3:])
        hs, *sample_states = _mixer_sample(l, hs, state_delta, cbuf_all, pbuf_all, tuple(sample_states), *mix_w)
        xp, xs = _ffn_ln(l, hp.reshape(bsz * seq, D_MODEL), hs, *f2, ln3g, ln3b)
        dp.append(s_p); cp.append(c_p); pp.append(p_p)
    delta_sample, conv_sample, pool_sample = sample_states
    return (xp.reshape(bsz, seq, D_MODEL), xs.reshape(dec_b, 1, D_MODEL),
            jnp.stack(dp), jnp.stack(cp), jnp.stack(pp),
            delta_sample, jnp.transpose(conv_sample, (0, 2, 1, 3)), jnp.transpose(pool_sample, (0, 2, 1, 3)))
```

```python
import functools

import jax
import jax.numpy as jnp
from jax import lax
from jax.experimental import pallas as pl
from jax.experimental.pallas import tpu as pltpu

F32 = jnp.float32
BF16 = jnp.bfloat16

D_MODEL = 1024
DEPTH = 4
DN_HEADS = 4
HEAD_DIM = 128
DN_DIM = DN_HEADS * HEAD_DIM
POOL_DIM = 512
QKV_DIM = 3 * DN_DIM
CONV_W = 4
POOL_WINDOWS = (2, 4, 8, 16)
POOL_GROUP_DIM = POOL_DIM // len(POOL_WINDOWS)
POOL_BUF = max(POOL_WINDOWS) - 1
D_FF = 2816
PAST_LEN = 16384
DN_ALPHA = (2.0 * DEPTH) ** 0.25
LN_EPS = 1e-5
RMS_EPS = 1e-6
L2_EPS = 1e-6

N_GATE = 2 * DN_HEADS
Z_OFF = QKV_DIM
GATE_OFF = QKV_DIM + DN_DIM
POOL_OFF = GATE_OFF + N_GATE
IN_DIM = POOL_OFF + POOL_DIM
GATE_PAD = 128
DECAY_LANE = DN_HEADS

CHUNK = 128
INV_BASE = 8
FFN_ROWS = 1024
FFN_SUB_ROWS = 256
MIX_ROWS = 512
OUT_CHUNKS = 2
CAST_STEPS = 2
BF16_SUBLANES = 16
F32_SUBLANES = 8
CONV_HDR = F32_SUBLANES
POOL_HDR = -(-POOL_BUF // F32_SUBLANES) * F32_SUBLANES
MASKED_LOG = -1e30
SAMPLE_TILE = 8


def _dot(a, b):
    return jnp.dot(a.astype(BF16), b.astype(BF16), preferred_element_type=F32)


def _dot_nt(a, b):
    return lax.dot_general(a.astype(BF16), b.astype(BF16), (((1,), (1,)), ((), ())),
                           preferred_element_type=F32)


def _dot_tn(a, b):
    return lax.dot_general(a.astype(BF16), b.astype(BF16), (((0,), (0,)), ((), ())),
                           preferred_element_type=F32)


def _split_bf16(x):
    hi = x.astype(BF16)
    lo = (x - hi.astype(F32)).astype(BF16)
    return hi, lo


def _dot_exact_lhs(a_bf16, b):
    b1 = b.astype(BF16)
    r1 = b - b1.astype(F32)
    b2 = r1.astype(BF16)
    b3 = (r1 - b2.astype(F32)).astype(BF16)
    d = functools.partial(jnp.dot, preferred_element_type=F32)
    return d(a_bf16, b1) + (d(a_bf16, b2) + d(a_bf16, b3))


def _silu(x):
    return x * jax.nn.sigmoid(x)


def _softplus(x):
    return jnp.maximum(x, 0.0) + jnp.log1p(jnp.exp(-jnp.abs(x)))


def _layer_norm(y, g, b):
    mu = jnp.mean(y, axis=-1, keepdims=True)
    yc = y - mu
    var = jnp.mean(yc * yc, axis=-1, keepdims=True)
    return yc * lax.rsqrt(var + LN_EPS) * g + b


def _l2norm(x, scale=None):
    inv = lax.rsqrt(jnp.sum(x * x, axis=-1, keepdims=True) + L2_EPS)
    return x * (inv if scale is None else inv * scale)


def _ffn_ln_rows(x, wg_ref, wu_ref, wd_ref, g_ref, b_ref):
    xb = x.astype(BF16)
    gate = jnp.dot(xb, wg_ref[...], preferred_element_type=F32)
    up = jnp.dot(xb, wu_ref[...], preferred_element_type=F32)
    act = (_silu(gate) * up).astype(BF16)
    ff = jnp.dot(act, wd_ref[...], preferred_element_type=F32)
    return _layer_norm(DN_ALPHA * x + 0.5 * ff, g_ref[...], b_ref[...])


def _ffn_ln_kernel(x_ref, xs_ref, wg_ref, wu_ref, wd_ref, g_ref, b_ref, o_ref, os_ref):
    weights = (wg_ref, wu_ref, wd_ref, g_ref, b_ref)
    tm = x_ref.shape[0]
    for r in range(tm // FFN_SUB_ROWS):
        rows = slice(r * FFN_SUB_ROWS, (r + 1) * FFN_SUB_ROWS)
        o_ref[rows, :] = _ffn_ln_rows(x_ref[rows, :], *weights)

    @pl.when(pl.program_id(0) == pl.num_programs(0) - 1)
    def _():
        os_ref[...] = _ffn_ln_rows(xs_ref[...], *weights)


def _resident(shape):
    zeros = (0,) * len(shape)
    return pl.BlockSpec(shape, lambda *_: zeros, pipeline_mode=pl.Buffered(1))


def _layer_resident(layer, shape):
    index = (layer,) + (0,) * len(shape)
    return pl.BlockSpec((None,) + tuple(shape), lambda *_: index, pipeline_mode=pl.Buffered(1))


def _ffn_ln(layer, x, xs, wg, wu, wd, g, b):
    rows = x.shape[0]
    tm = FFN_ROWS
    assert rows % tm == 0 and tm % FFN_SUB_ROWS == 0
    return pl.pallas_call(
        _ffn_ln_kernel,
        out_shape=(jax.ShapeDtypeStruct((rows, D_MODEL), F32), jax.ShapeDtypeStruct(xs.shape, F32)),
        grid=(rows // tm,),
        in_specs=[
            pl.BlockSpec((tm, D_MODEL), lambda i: (i, 0)),
            _resident(xs.shape),
            _resident((D_MODEL, D_FF)),
            _resident((D_MODEL, D_FF)),
            _resident((D_FF, D_MODEL)),
            _layer_resident(layer, (1, D_MODEL)),
            _layer_resident(layer, (1, D_MODEL)),
        ],
        out_specs=(pl.BlockSpec((tm, D_MODEL), lambda i: (i, 0)),
                   pl.BlockSpec(xs.shape, lambda i: (0, 0))),
        compiler_params=pltpu.CompilerParams(dimension_semantics=("arbitrary",)),
        name="ffn_ln",
    )(x, xs, wg, wu, wd, g, b)


def _gates(ba, alog, dtb):
    beta = jax.nn.sigmoid(ba)
    g = -jnp.exp(alog) * _softplus(ba + dtb)
    return beta, g


def _qkv_heads(conv_out):
    act = _silu(conv_out)
    qs, ks = [], []
    for h in range(DN_HEADS):
        lo = h * HEAD_DIM
        qs.append(_l2norm(act[:, lo:lo + HEAD_DIM], HEAD_DIM ** -0.5))
        ks.append(_l2norm(act[:, DN_DIM + lo:DN_DIM + lo + HEAD_DIM]))
    return jnp.concatenate(qs, -1), jnp.concatenate(ks, -1), act[:, 2 * DN_DIM:]


def _gated_out_norm(o, z, onorm_g):
    outs = []
    for h in range(DN_HEADS):
        sl = slice(h * HEAD_DIM, (h + 1) * HEAD_DIM)
        oh = o[:, sl]
        oh = oh * lax.rsqrt(jnp.mean(oh * oh, axis=-1, keepdims=True) + RMS_EPS)
        outs.append(oh * onorm_g * _silu(z[:, sl]))
    return jnp.concatenate(outs, -1)


def _pool_project(d_groups, poolw_ref, pscale):
    ys = [_dot(d, poolw_ref[gi]) for gi, d in enumerate(d_groups)]
    return jnp.concatenate(ys, -1) * pscale


def _inv_unit_lower(ms, row, col, size):
    eye = jnp.where(row == col, 1.0, 0.0).astype(F32)
    base_bits = INV_BASE.bit_length() - 1
    base_mask = (row >> base_bits) == (col >> base_bits)
    ps = [jnp.where(base_mask, m, 0.0) for m in ms]
    xs = [eye - p for p in ps]
    ps = [_dot(p, p) for p in ps]
    for step in range(base_bits - 1):
        if step < base_bits - 2:
            xps = [_dot(jnp.concatenate([x, p], axis=0), p) for x, p in zip(xs, ps)]
            xs = [x + xp[:size] for x, xp in zip(xs, xps)]
            ps = [xp[size:] for xp in xps]
        else:
            xs = [x + _dot(x, p) for x, p in zip(xs, ps)]
    bits = base_bits
    while (1 << bits) < size:
        same_pair = (row >> (bits + 1)) == (col >> (bits + 1))
        lower_left = (((row >> bits) & 1) == 1) & (((col >> bits) & 1) == 0)
        mask = same_pair & lower_left
        ys = [_dot(jnp.where(mask, m, 0.0), x) for m, x in zip(ms, xs)]
        xs = [x - _dot(x, y) for x, y in zip(xs, ys)]
        bits += 1
    return xs


def _mixer_prompt_kernel(n_cast, n_zero, h_ref, win_ref, convw_ref, alog_ref, dtb_ref, onorm_ref, poolw_ref,
                         pscale_ref, wout_ref, g2_ref, b2_ref, *refs):
    cast_src = refs[:n_cast]
    y_ref, sfin_ref, cfin_ref, pfin_ref = refs[n_cast:n_cast + 4]
    cast_dst = refs[n_cast + 4:2 * n_cast + 4]
    zero_dst = refs[2 * n_cast + 4:2 * n_cast + 4 + n_zero]
    s_scr, ext_scr, pext_scr, qkv_scr, beta_scr, g_scr, z_scr, opool_scr, pw_scr, gw_scr = (
        refs[2 * n_cast + 4 + n_zero:])
    tl = h_ref.shape[0]
    l = pl.program_id(1)
    n_l = pl.num_programs(1)

    @pl.when((pl.program_id(0) * n_l + l) % CAST_STEPS == 0)
    def _():
        for src, dst in zip(cast_src, cast_dst):
            dst[...] = src[...].astype(BF16)

    for dst in zero_dst:
        dst[...] = jnp.zeros(dst.shape, dst.dtype)

    @pl.when(l == 0)
    def _():
        s_scr[...] = jnp.zeros_like(s_scr)
        ext_scr[:, 0:CONV_HDR, :] = jnp.zeros((QKV_DIM // HEAD_DIM, CONV_HDR, HEAD_DIM), F32)
        pext_scr[:, 0:POOL_HDR, :] = jnp.zeros((len(POOL_WINDOWS), POOL_HDR, POOL_GROUP_DIM), F32)

    hb = h_ref[...].astype(BF16)

    @pl.when((pl.program_id(0) == 0) & (l == 0))
    def _():
        pw_scr[...] = win_ref[:, POOL_OFF:POOL_OFF + POOL_DIM]
        gw_scr[...] = jnp.concatenate([win_ref[:, GATE_OFF:GATE_OFF + N_GATE],
                                       jnp.zeros((D_MODEL, GATE_PAD - N_GATE), BF16)], axis=-1)

    def project(w_ref, lo, width):
        return jnp.dot(hb, w_ref[:, lo:lo + width], preferred_element_type=F32)

    def conv_slab(t, x_t):
        lanes = slice(t * HEAD_DIM, (t + 1) * HEAD_DIM)
        first = CONV_HDR - (CONV_W - 1)
        ext_scr[t, CONV_HDR:CONV_HDR + tl, :] = x_t
        conv = convw_ref[CONV_W - 1:CONV_W, lanes] * x_t
        for i in range(CONV_W - 1):
            conv = conv + convw_ref[i:i + 1, lanes] * ext_scr[t, first + i:first + i + tl, :]
        cfin_ref[:, lanes] = ext_scr[t, tl + first:tl + CONV_HDR, :]
        ext_scr[t, 0:CONV_HDR, :] = ext_scr[t, tl:tl + CONV_HDR, :]
        act = _silu(conv)
        if t < DN_HEADS:
            act = _l2norm(act, HEAD_DIM ** -0.5)
        elif t < 2 * DN_HEADS:
            act = _l2norm(act)
        qkv_scr[t] = act

    def pool_slab(gi, tok):
        w = POOL_WINDOWS[gi]
        lanes = slice(gi * POOL_GROUP_DIM, (gi + 1) * POOL_GROUP_DIM)
        pext_scr[gi, POOL_HDR:POOL_HDR + tl, :] = tok
        s = tok
        for i in range(1, w):
            s = s + pext_scr[gi, POOL_HDR - i:POOL_HDR - i + tl, :]
        pos = l * tl + lax.broadcasted_iota(jnp.int32, (tl, 1), 0)
        cnt = jnp.minimum(w, pos + 1).astype(F32)
        pfin_ref[:, lanes] = pext_scr[gi, tl + POOL_HDR - POOL_BUF:tl + POOL_HDR, :]
        pext_scr[gi, 0:POOL_HDR, :] = pext_scr[gi, tl:tl + POOL_HDR, :]
        return s / cnt - tok

    group = 2 * HEAD_DIM
    for j in range(QKV_DIM // group):
        y = project(win_ref, j * group, group)
        conv_slab(2 * j, y[:, :HEAD_DIM])
        conv_slab(2 * j + 1, y[:, HEAD_DIM:])
    beta_t, g_t = _gates(project(gw_scr, 0, GATE_PAD), alog_ref[...], dtb_ref[...])
    beta_scr[...] = beta_t
    g_scr[...] = g_t
    d_groups = []
    for j in range(POOL_DIM // group):
        y = project(pw_scr, j * group, group)
        d_groups.append(pool_slab(2 * j, y[:, :POOL_GROUP_DIM]))
        d_groups.append(pool_slab(2 * j + 1, y[:, POOL_GROUP_DIM:]))
    opool_scr[...] = _pool_project(d_groups, poolw_ref, pscale_ref[...])
    z_scr[...] = project(win_ref, Z_OFF, DN_DIM)

    c = CHUNK
    row = lax.broadcasted_iota(jnp.int32, (c, c), 0)
    col = lax.broadcasted_iota(jnp.int32, (c, c), 1)
    incl = row >= col
    strict = row > col
    tri = jnp.where(incl, 1.0, 0.0).astype(BF16)

    n_chunks = tl // c
    chains = [(ci, hd) for ci in range(n_chunks) for hd in range(DN_HEADS)]
    gc_alls, gc_ts, beta_alls = [], [], []
    for ci in range(n_chunks):
        rows = slice(ci * c, (ci + 1) * c)
        gc_all = _dot_exact_lhs(tri, g_scr[rows, :])
        gc_alls.append(gc_all)
        gc_ts.append(gc_all.T)
        beta_alls.append(beta_scr[rows, :])
    qs, ks, betas, egs, decs, rhss, k_ends, g_ends = [], [], [], [], [], [], [], []
    for ci, hd in chains:
        rows = slice(ci * c, (ci + 1) * c)
        sl = slice(hd * HEAD_DIM, (hd + 1) * HEAD_DIM)
        lane = DECAY_LANE + hd
        qh, kh, vh = qkv_scr[hd, rows, :], qkv_scr[DN_HEADS + hd, rows, :], qkv_scr[2 * DN_HEADS + hd, rows, :]
        beta = beta_alls[ci][:, hd:hd + 1]
        gc_col = gc_alls[ci][:, lane:lane + 1]
        gc_row = gc_ts[ci][lane:lane + 1, :]
        gc_last = gc_alls[ci][c - 1:c, lane:lane + 1]
        eg = jnp.exp(gc_col)
        decs.append(jnp.exp(jnp.where(incl, gc_col - gc_row, MASKED_LOG)))
        rhss.append(jnp.concatenate([beta * vh, (beta * eg) * kh], axis=-1))
        k_ends.append(kh * jnp.exp(gc_last - gc_col))
        g_ends.append(jnp.exp(gc_last))
        qs.append(qh); ks.append(kh); betas.append(beta); egs.append(eg)
    kqs = [_dot_nt(jnp.concatenate([kh, qh], axis=0), kh) for kh, qh in zip(ks, qs)]
    ms = [jnp.where(strict, beta * kq[:c] * dec, 0.0) for beta, kq, dec in zip(betas, kqs, decs)]
    qks = [kq[c:] * dec for kq, dec in zip(kqs, decs)]
    t_invs = _inv_unit_lower(ms, row, col, c)
    sols = [_dot(t_inv, rhs) for t_inv, rhs in zip(t_invs, rhss)]
    q_decs = [qh * eg for qh, eg in zip(qs, egs)]

    s_cur = [s_scr[hd] for hd in range(DN_HEADS)]
    o_chunks = []
    for ci in range(n_chunks):
        idx = [ci * DN_HEADS + hd for hd in range(DN_HEADS)]
        wss = [_dot(jnp.concatenate([sols[i][:, HEAD_DIM:], q_decs[i]], axis=0), s) for i, s in zip(idx, s_cur)]
        us = [sols[i][:, :HEAD_DIM] - ws[:c] for i, ws in zip(idx, wss)]
        outs = [ws[c:] + _dot(qks[i], u) for i, ws, u in zip(idx, wss, us)]
        s_cur = [g_ends[i] * s + _dot_tn(k_ends[i], u) for i, s, u in zip(idx, s_cur, us)]
        o_chunks.append(jnp.concatenate(outs, axis=-1))
        if len(o_chunks) == OUT_CHUNKS or ci == n_chunks - 1:
            rows = slice((ci + 1 - len(o_chunks)) * c, (ci + 1) * c)
            o_dn = _gated_out_norm(jnp.concatenate(o_chunks, axis=0), z_scr[rows, :], onorm_ref[...])
            mix = _dot(jnp.concatenate([o_dn, opool_scr[rows, :]], axis=-1), wout_ref[...])
            y_ref[rows, :] = _layer_norm(DN_ALPHA * h_ref[rows, :] + mix, g2_ref[...], b2_ref[...])
            o_chunks = []
    for hd in range(DN_HEADS):
        s_scr[hd] = s_cur[hd]

    @pl.when(l == n_l - 1)
    def _():
        sfin_ref[...] = s_scr[...]


def _mixer_weight_specs(layer):
    return [
        _layer_resident(layer, (D_MODEL, IN_DIM)),
        _layer_resident(layer, (CONV_W, QKV_DIM)),
        _layer_resident(layer, (1, GATE_PAD)),
        _layer_resident(layer, (1, GATE_PAD)),
        _layer_resident(layer, (1, HEAD_DIM)),
        _layer_resident(layer, (len(POOL_WINDOWS), POOL_GROUP_DIM, POOL_GROUP_DIM)),
        _layer_resident(layer, (1, POOL_DIM)),
        _layer_resident(layer, (DN_DIM + POOL_DIM, D_MODEL)),
        _layer_resident(layer, (1, D_MODEL)),
        _layer_resident(layer, (1, D_MODEL)),
    ]


def _mixer_prompt(layer, h, weights, cast_jobs, zero_shapes=()):
    bsz, seq, _ = h.shape
    tl = MIX_ROWS
    n_l = seq // tl
    assert seq % tl == 0 and tl % CHUNK == 0 and (bsz * n_l) % CAST_STEPS == 0
    n_blocks = bsz * n_l // CAST_STEPS
    out_shape = [
        jax.ShapeDtypeStruct((bsz, seq, D_MODEL), F32),
        jax.ShapeDtypeStruct((bsz, DN_HEADS, HEAD_DIM, HEAD_DIM), F32),
        jax.ShapeDtypeStruct((bsz, CONV_W - 1, QKV_DIM), F32),
        jax.ShapeDtypeStruct((bsz, POOL_BUF, POOL_DIM), F32),
    ]
    out_specs = [
        pl.BlockSpec((None, tl, D_MODEL), lambda b, l: (b, l, 0)),
        pl.BlockSpec((None, DN_HEADS, HEAD_DIM, HEAD_DIM), lambda b, l: (b, 0, 0, 0)),
        pl.BlockSpec((None, CONV_W - 1, QKV_DIM), lambda b, l: (b, 0, 0)),
        pl.BlockSpec((None, POOL_BUF, POOL_DIM), lambda b, l: (b, 0, 0)),
    ]
    cast_specs = []
    for w_all, w_layer in cast_jobs:
        _, n_rows, n_cols = w_all.shape
        assert n_rows % (n_blocks * BF16_SUBLANES) == 0
        blk = n_rows // n_blocks
        cast_specs.append(pl.BlockSpec((None, blk, n_cols),
                                       lambda b, l, w_layer=w_layer: (w_layer, (b * n_l + l) // CAST_STEPS, 0)))
        out_shape.append(jax.ShapeDtypeStruct((n_rows, n_cols), BF16))
        out_specs.append(pl.BlockSpec((blk, n_cols), lambda b, l: ((b * n_l + l) // CAST_STEPS, 0)))
    for shape in zero_shapes:
        assert shape[0] % (bsz * n_l) == 0
        rest = tuple(shape[1:])
        out_shape.append(jax.ShapeDtypeStruct(tuple(shape), F32))
        out_specs.append(pl.BlockSpec((shape[0] // (bsz * n_l),) + rest,
                                      lambda b, l, nd=len(rest): (b * n_l + l,) + (0,) * nd))
    return pl.pallas_call(
        functools.partial(_mixer_prompt_kernel, len(cast_jobs), len(zero_shapes)),
        out_shape=tuple(out_shape),
        grid=(bsz, n_l),
        in_specs=[
            pl.BlockSpec((None, tl, D_MODEL), lambda b, l: (b, l, 0)),
        ] + _mixer_weight_specs(layer) + cast_specs,
        out_specs=tuple(out_specs),
        scratch_shapes=[
            pltpu.VMEM((DN_HEADS, HEAD_DIM, HEAD_DIM), F32),
            pltpu.VMEM((QKV_DIM // HEAD_DIM, tl + CONV_HDR, HEAD_DIM), F32),
            pltpu.VMEM((len(POOL_WINDOWS), tl + POOL_HDR, POOL_GROUP_DIM), F32),
            pltpu.VMEM((QKV_DIM // HEAD_DIM, tl, HEAD_DIM), F32),
            pltpu.VMEM((tl, GATE_PAD), F32),
            pltpu.VMEM((tl, GATE_PAD), F32),
            pltpu.VMEM((tl, DN_DIM), F32),
            pltpu.VMEM((tl, POOL_DIM), F32),
            pltpu.VMEM((D_MODEL, POOL_DIM), BF16),
            pltpu.VMEM((D_MODEL, GATE_PAD), BF16),
        ],
        compiler_params=pltpu.CompilerParams(dimension_semantics=("arbitrary", "arbitrary")),
        name="mixer_prompt",
    )(h, *weights, *[w_all for w_all, _ in cast_jobs])


N_SAMPLE_STATES = 3

def _mixer_sample_kernel(h_ref, s_ref, cbuf_ref, pbuf_ref, win_ref, convw_ref, alog_ref, dtb_ref, onorm_ref,
                         poolw_ref, pscale_ref, wout_ref, g2_ref, b2_ref, *refs):
    refs = refs[N_SAMPLE_STATES:]
    y_ref, snew_ref, cnew_ref, pnew_ref = refs[:4]
    q_scr, k_scr, v_scr, beta_scr, eg_scr, z_scr, opool_scr, o_scr = refs[4:]
    i = pl.program_id(0)
    n_i = pl.num_programs(0)

    @pl.when(i == 0)
    def _():
        proj = jnp.dot(h_ref[...].astype(BF16), win_ref[...], preferred_element_type=F32)
        qkv = proj[:, 0:QKV_DIM]
        conv = convw_ref[CONV_W - 1:CONV_W, :] * qkv
        for j in range(CONV_W - 1):
            conv = conv + convw_ref[j:j + 1, :] * cbuf_ref[j]
        for j in range(CONV_W - 2):
            cnew_ref[j] = cbuf_ref[j + 1]
        cnew_ref[CONV_W - 2] = qkv
        q, k, v = _qkv_heads(conv)
        q_scr[...] = q
        k_scr[...] = k
        v_scr[...] = v
        ba = jnp.concatenate([proj[:, GATE_OFF:GATE_OFF + N_GATE],
                              jnp.zeros((proj.shape[0], GATE_PAD - N_GATE), F32)], axis=-1)
        beta_t, g_t = _gates(ba, alog_ref[...], dtb_ref[...])
        beta_scr[...] = beta_t
        eg_scr[...] = jnp.exp(g_t)
        z_scr[...] = proj[:, Z_OFF:Z_OFF + DN_DIM]

        p = proj[:, POOL_OFF:POOL_OFF + POOL_DIM]
        d_groups = []
        for gi, w in enumerate(POOL_WINDOWS):
            lo = gi * POOL_GROUP_DIM
            tok = p[:, lo:lo + POOL_GROUP_DIM]
            s = tok
            for back in range(1, w):
                s = s + pbuf_ref[POOL_BUF - back, :, lo:lo + POOL_GROUP_DIM]
            cnt = float(min(w, PAST_LEN + 1))
            d_groups.append(s / cnt - tok)
        opool_scr[...] = _pool_project(d_groups, poolw_ref, pscale_ref[...])
        for j in range(POOL_BUF - 1):
            pnew_ref[j] = pbuf_ref[j + 1]
        pnew_ref[POOL_BUF - 1] = p

    rows = pl.ds(pl.multiple_of(i * SAMPLE_TILE, SAMPLE_TILE), SAMPLE_TILE)
    q_t, k_t, v_t = q_scr[rows, :], k_scr[rows, :], v_scr[rows, :]
    beta_t, a_t = beta_scr[rows, :], eg_scr[rows, :]
    eye = jnp.where(lax.broadcasted_iota(jnp.int32, (HEAD_DIM, HEAD_DIM), 0)
                    == lax.broadcasted_iota(jnp.int32, (HEAD_DIM, HEAD_DIM), 1), 1.0, 0.0).astype(F32)
    pad_rows = jnp.zeros((F32_SUBLANES - 2, HEAD_DIM), F32)
    o_heads = [[] for _ in range(SAMPLE_TILE)]
    for hd in range(DN_HEADS):
        sl = slice(hd * HEAD_DIM, (hd + 1) * HEAD_DIM)
        for j in range(SAMPLE_TILE):
            qh = q_t[j:j + 1, sl]
            kh = k_t[j:j + 1, sl]
            vh = v_t[j:j + 1, sl]
            beta = beta_t[j:j + 1, hd:hd + 1]
            a = a_t[j:j + 1, DECAY_LANE + hd:DECAY_LANE + hd + 1]
            s_old = s_ref[j, hd]
            kq = jnp.concatenate([kh, qh, pad_rows], axis=0)
            kq_hi, kq_lo = _split_bf16(kq)
            s_hi, s_lo = _split_bf16(s_old)
            mv = functools.partial(jnp.dot, preferred_element_type=F32)
            kqs = mv(kq_hi, s_hi) + (mv(kq_hi, s_lo) + mv(kq_lo, s_hi))
            ks, qs = kqs[0:1], kqs[1:2]
            k_col = jnp.sum(eye * kh, axis=-1, keepdims=True)
            u = beta * vh - (beta * a) * ks
            qk = jnp.sum(qh * kh, axis=-1, keepdims=True)
            o_heads[j].append(a * qs + qk * u)
            snew_ref[j, hd] = a * s_old + k_col * u
    o_scr[rows, :] = jnp.concatenate([jnp.concatenate(o_h, axis=-1) for o_h in o_heads], axis=0)

    @pl.when(i == n_i - 1)
    def _():
        o_dn = _gated_out_norm(o_scr[...], z_scr[...], onorm_ref[...])
        mix = _dot(jnp.concatenate([o_dn, opool_scr[...]], axis=-1), wout_ref[...])
        y_ref[...] = _layer_norm(DN_ALPHA * h_ref[...] + mix, g2_ref[...], b2_ref[...])


def _mixer_sample(layer, h, s_all, cbuf_all, pbuf_all, new_states, *weights):
    bsz = h.shape[0]
    assert bsz % SAMPLE_TILE == 0 and len(new_states) == N_SAMPLE_STATES
    cshape = (CONV_W - 1, bsz, QKV_DIM)
    pshape = (POOL_BUF, bsz, POOL_DIM)
    assert cbuf_all.shape[1:] == cshape and pbuf_all.shape[1:] == pshape
    out_shape = (
        jax.ShapeDtypeStruct((bsz, D_MODEL), F32),
        jax.ShapeDtypeStruct(s_all.shape, F32),
        jax.ShapeDtypeStruct(cbuf_all.shape, F32),
        jax.ShapeDtypeStruct(pbuf_all.shape, F32),
    )
    state_spec = pl.BlockSpec((None, SAMPLE_TILE, DN_HEADS, HEAD_DIM, HEAD_DIM), lambda i: (layer, i, 0, 0, 0))
    n_in = 4 + 10
    return pl.pallas_call(
        _mixer_sample_kernel,
        out_shape=out_shape,
        grid=(bsz // SAMPLE_TILE,),
        in_specs=[
            _resident((bsz, D_MODEL)),
            state_spec,
            _layer_resident(layer, cshape),
            _layer_resident(layer, pshape),
        ] + _mixer_weight_specs(layer) + [pl.BlockSpec(memory_space=pl.ANY)] * len(new_states),
        out_specs=(
            pl.BlockSpec((bsz, D_MODEL), lambda i: (0, 0)),
            state_spec,
            pl.BlockSpec((None,) + cshape, lambda i: (layer, 0, 0, 0)),
            pl.BlockSpec((None,) + pshape, lambda i: (layer, 0, 0, 0)),
        ),
        input_output_aliases={n_in + k: 1 + k for k in range(len(new_states))},
        scratch_shapes=[
            pltpu.VMEM((bsz, DN_DIM), F32),
            pltpu.VMEM((bsz, DN_DIM), F32),
            pltpu.VMEM((bsz, DN_DIM), F32),
            pltpu.VMEM((bsz, GATE_PAD), F32),
            pltpu.VMEM((bsz, GATE_PAD), F32),
            pltpu.VMEM((bsz, DN_DIM), F32),
            pltpu.VMEM((bsz, POOL_DIM), F32),
            pltpu.VMEM((bsz, DN_DIM), F32),
        ],
        compiler_params=pltpu.CompilerParams(dimension_semantics=("arbitrary",)),
        name="mixer_sample",
    )(h, s_all, cbuf_all, pbuf_all, *weights, *new_states)


def _pad_gate_param(v):
    out = jnp.zeros((v.shape[0], 1, GATE_PAD), F32)
    return out.at[:, 0, DECAY_LANE:DECAY_LANE + DN_HEADS].set(v.astype(F32))


def kernel(x_prompt, x_sample, state_delta, state_conv, state_pool, ln1_g, ln1_b, ffn1_w_gate, ffn1_w_up, ffn1_w_down, w_in, conv_w, a_log, dt_bias, onorm_g, pool_w, pool_scale, w_out, ln2_g, ln2_b, ffn2_w_gate, ffn2_w_up, ffn2_w_down, ln3_g, ln3_b):
    bsz, seq, _ = x_prompt.shape
    dec_b = x_sample.shape[0]

    win = w_in.astype(BF16)
    wout = w_out.astype(BF16)
    poolw = pool_w.astype(BF16)
    ffn1_stacks = (ffn1_w_gate, ffn1_w_up, ffn1_w_down)
    ffn2_stacks = (ffn2_w_gate, ffn2_w_up, ffn2_w_down)
    f1 = tuple(w[0].astype(BF16) for w in ffn1_stacks)
    alog = _pad_gate_param(a_log)
    dtb = _pad_gate_param(dt_bias)
    row = lambda v: v.reshape(DEPTH, 1, -1)
    ln1g, ln1b, ln2g, ln2b, ln3g, ln3b = map(row, (ln1_g, ln1_b, ln2_g, ln2_b, ln3_g, ln3_b))
    onorm = row(onorm_g)
    pscale = row(pool_scale)
    mix_w = (win, conv_w, alog, dtb, onorm, poolw, pscale, wout, ln2g, ln2b)

    xp = x_prompt.reshape(bsz * seq, D_MODEL)
    xs = x_sample.reshape(dec_b, D_MODEL)
    dp, cp, pp = [], [], []
    cbuf_all = jnp.transpose(state_conv, (0, 2, 1, 3))
    pbuf_all = jnp.transpose(state_pool, (0, 2, 1, 3))
    state_shapes = [state_delta.shape, cbuf_all.shape, pbuf_all.shape]
    flat_shapes = [(DEPTH * dec_b,) + state_delta.shape[2:],
                   (DEPTH * (CONV_W - 1) * dec_b, QKV_DIM), (DEPTH * POOL_BUF * dec_b, POOL_DIM)]
    for l in range(DEPTH):
        hp, hs = _ffn_ln(l, xp, xs, *f1, ln1g, ln1b)
        cast_jobs = [(w, l) for w in ffn2_stacks]
        if l + 1 < DEPTH:
            cast_jobs += [(w, l + 1) for w in ffn1_stacks]
        hp, s_p, c_p, p_p, *extra = _mixer_prompt(l, hp.reshape(bsz, seq, D_MODEL), mix_w, cast_jobs,
                                                  flat_shapes if l == 0 else ())
        cast = extra[:len(cast_jobs)]
        if l == 0:
            sample_states = tuple(z.reshape(s) for z, s in zip(extra[len(cast_jobs):], state_shapes))
        f2, f1 = tuple(cast[:3]), tuple(cast[3:])
        hs, *sample_states = _mixer_sample(l, hs, state_delta, cbuf_all, pbuf_all, tuple(sample_states), *mix_w)
        xp, xs = _ffn_ln(l, hp.reshape(bsz * seq, D_MODEL), hs, *f2, ln3g, ln3b)
        dp.append(s_p); cp.append(c_p); pp.append(p_p)
    delta_sample, conv_sample, pool_sample = sample_states
    return (xp.reshape(bsz, seq, D_MODEL), xs.reshape(dec_b, 1, D_MODEL),
            jnp.stack(dp), jnp.stack(cp), jnp.stack(pp),
            delta_sample, jnp.transpose(conv_sample, (0, 2, 1, 3)), jnp.transpose(pool_sample, (0, 2, 1, 3)))
```

```python
import functools

import jax
import jax.numpy as jnp
from jax import lax
from jax.experimental import pallas as pl
from jax.experimental.pallas import tpu as pltpu

F32 = jnp.float32
BF16 = jnp.bfloat16

D_MODEL = 1024
DEPTH = 4
DN_HEADS = 4
HEAD_DIM = 128
DN_DIM = DN_HEADS * HEAD_DIM
POOL_DIM = 512
QKV_DIM = 3 * DN_DIM
CONV_W = 4
POOL_WINDOWS = (2, 4, 8, 16)
POOL_GROUP_DIM = POOL_DIM // len(POOL_WINDOWS)
POOL_BUF = max(POOL_WINDOWS) - 1
D_FF = 2816
PAST_LEN = 16384
DN_ALPHA = (2.0 * DEPTH) ** 0.25
LN_EPS = 1e-5
RMS_EPS = 1e-6
L2_EPS = 1e-6

N_GATE = 2 * DN_HEADS
Z_OFF = QKV_DIM
GATE_OFF = QKV_DIM + DN_DIM
POOL_OFF = GATE_OFF + N_GATE
IN_DIM = POOL_OFF + POOL_DIM
GATE_PAD = 128
DECAY_LANE = DN_HEADS

CHUNK = 128
INV_BASE = 8
FFN_ROWS = 1024
FFN_SUB_ROWS = 256
MIX_ROWS = 512
OUT_CHUNKS = 2
CAST_STEPS = 2
BF16_SUBLANES = 16
F32_SUBLANES = 8
CONV_HDR = F32_SUBLANES
POOL_HDR = -(-POOL_BUF // F32_SUBLANES) * F32_SUBLANES
MASKED_LOG = -1e30
SAMPLE_TILE = 16


def _dot(a, b):
    return jnp.dot(a.astype(BF16), b.astype(BF16), preferred_element_type=F32)


def _dot_nt(a, b):
    return lax.dot_general(a.astype(BF16), b.astype(BF16), (((1,), (1,)), ((), ())),
                           preferred_element_type=F32)


def _dot_tn(a, b):
    return lax.dot_general(a.astype(BF16), b.astype(BF16), (((0,), (0,)), ((), ())),
                           preferred_element_type=F32)


def _split_bf16(x):
    hi = x.astype(BF16)
    lo = (x - hi.astype(F32)).astype(BF16)
    return hi, lo


def _dot_exact_lhs(a_bf16, b):
    b1 = b.astype(BF16)
    r1 = b - b1.astype(F32)
    b2 = r1.astype(BF16)
    b3 = (r1 - b2.astype(F32)).astype(BF16)
    d = functools.partial(jnp.dot, preferred_element_type=F32)
    return d(a_bf16, b1) + (d(a_bf16, b2) + d(a_bf16, b3))


def _silu(x):
    return x * jax.nn.sigmoid(x)


def _softplus(x):
    return jnp.maximum(x, 0.0) + jnp.log1p(jnp.exp(-jnp.abs(x)))


def _layer_norm(y, g, b):
    mu = jnp.mean(y, axis=-1, keepdims=True)
    yc = y - mu
    var = jnp.mean(yc * yc, axis=-1, keepdims=True)
    return yc * lax.rsqrt(var + LN_EPS) * g + b


def _l2norm(x, scale=None):
    inv = lax.rsqrt(jnp.sum(x * x, axis=-1, keepdims=True) + L2_EPS)
    return x * (inv if scale is None else inv * scale)


def _ffn_ln_rows(x, wg_ref, wu_ref, wd_ref, g_ref, b_ref):
    xb = x.astype(BF16)
    gate = jnp.dot(xb, wg_ref[...], preferred_element_type=F32)
    up = jnp.dot(xb, wu_ref[...], preferred_element_type=F32)
    act = (_silu(gate) * up).astype(BF16)
    ff = jnp.dot(act, wd_ref[...], preferred_element_type=F32)
    return _layer_norm(DN_ALPHA * x + 0.5 * ff, g_ref[...], b_ref[...])


def _ffn_ln_kernel(x_ref, xs_ref, wg_ref, wu_ref, wd_ref, g_ref, b_ref, o_ref, os_ref):
    weights = (wg_ref, wu_ref, wd_ref, g_ref, b_ref)
    tm = x_ref.shape[0]
    for r in range(tm // FFN_SUB_ROWS):
        rows = slice(r * FFN_SUB_ROWS, (r + 1) * FFN_SUB_ROWS)
        o_ref[rows, :] = _ffn_ln_rows(x_ref[rows, :], *weights)

    @pl.when(pl.program_id(0) == pl.num_programs(0) - 1)
    def _():
        os_ref[...] = _ffn_ln_rows(xs_ref[...], *weights)


def _resident(shape):
    zeros = (0,) * len(shape)
    return pl.BlockSpec(shape, lambda *_: zeros, pipeline_mode=pl.Buffered(1))


def _layer_resident(layer, shape):
    index = (layer,) + (0,) * len(shape)
    return pl.BlockSpec((None,) + tuple(shape), lambda *_: index, pipeline_mode=pl.Buffered(1))


def _ffn_ln(layer, x, xs, wg, wu, wd, g, b):
    rows = x.shape[0]
    tm = FFN_ROWS
    assert rows % tm == 0 and tm % FFN_SUB_ROWS == 0
    return pl.pallas_call(
        _ffn_ln_kernel,
        out_shape=(jax.ShapeDtypeStruct((rows, D_MODEL), F32), jax.ShapeDtypeStruct(xs.shape, F32)),
        grid=(rows // tm,),
        in_specs=[
            pl.BlockSpec((tm, D_MODEL), lambda i: (i, 0)),
            _resident(xs.shape),
            _resident((D_MODEL, D_FF)),
            _resident((D_MODEL, D_FF)),
            _resident((D_FF, D_MODEL)),
            _layer_resident(layer, (1, D_MODEL)),
            _layer_resident(layer, (1, D_MODEL)),
        ],
        out_specs=(pl.BlockSpec((tm, D_MODEL), lambda i: (i, 0)),
                   pl.BlockSpec(xs.shape, lambda i: (0, 0))),
        compiler_params=pltpu.CompilerParams(dimension_semantics=("arbitrary",)),
        name="ffn_ln",
    )(x, xs, wg, wu, wd, g, b)


def _gates(ba, alog, dtb):
    beta = jax.nn.sigmoid(ba)
    g = -jnp.exp(alog) * _softplus(ba + dtb)
    return beta, g


def _qkv_heads(conv_out):
    act = _silu(conv_out)
    qs, ks = [], []
    for h in range(DN_HEADS):
        lo = h * HEAD_DIM
        qs.append(_l2norm(act[:, lo:lo + HEAD_DIM], HEAD_DIM ** -0.5))
        ks.append(_l2norm(act[:, DN_DIM + lo:DN_DIM + lo + HEAD_DIM]))
    return jnp.concatenate(qs, -1), jnp.concatenate(ks, -1), act[:, 2 * DN_DIM:]


def _gated_out_norm(o, z, onorm_g):
    outs = []
    for h in range(DN_HEADS):
        sl = slice(h * HEAD_DIM, (h + 1) * HEAD_DIM)
        oh = o[:, sl]
        oh = oh * lax.rsqrt(jnp.mean(oh * oh, axis=-1, keepdims=True) + RMS_EPS)
        outs.append(oh * onorm_g * _silu(z[:, sl]))
    return jnp.concatenate(outs, -1)


def _pool_project(d_groups, poolw_ref, pscale):
    ys = [_dot(d, poolw_ref[gi]) for gi, d in enumerate(d_groups)]
    return jnp.concatenate(ys, -1) * pscale


def _inv_unit_lower(ms, row, col, size):
    eye = jnp.where(row == col, 1.0, 0.0).astype(F32)
    base_bits = INV_BASE.bit_length() - 1
    base_mask = (row >> base_bits) == (col >> base_bits)
    ps = [jnp.where(base_mask, m, 0.0) for m in ms]
    xs = [eye - p for p in ps]
    ps = [_dot(p, p) for p in ps]
    for step in range(base_bits - 1):
        if step < base_bits - 2:
            xps = [_dot(jnp.concatenate([x, p], axis=0), p) for x, p in zip(xs, ps)]
            xs = [x + xp[:size] for x, xp in zip(xs, xps)]
            ps = [xp[size:] for xp in xps]
        else:
            xs = [x + _dot(x, p) for x, p in zip(xs, ps)]
    bits = base_bits
    while (1 << bits) < size:
        same_pair = (row >> (bits + 1)) == (col >> (bits + 1))
        lower_left = (((row >> bits) & 1) == 1) & (((col >> bits) & 1) == 0)
        mask = same_pair & lower_left
        ys = [_dot(jnp.where(mask, m, 0.0), x) for m, x in zip(ms, xs)]
        xs = [x - _dot(x, y) for x, y in zip(xs, ys)]
        bits += 1
    return xs


def _mixer_prompt_kernel(n_cast, n_zero, h_ref, win_ref, convw_ref, alog_ref, dtb_ref, onorm_ref, poolw_ref,
                         pscale_ref, wout_ref, g2_ref, b2_ref, *refs):
    cast_src = refs[:n_cast]
    y_ref, sfin_ref, cfin_ref, pfin_ref = refs[n_cast:n_cast + 4]
    cast_dst = refs[n_cast + 4:2 * n_cast + 4]
    zero_dst = refs[2 * n_cast + 4:2 * n_cast + 4 + n_zero]
    s_scr, ext_scr, pext_scr, qkv_scr, beta_scr, g_scr, z_scr, opool_scr, pw_scr, gw_scr = (
        refs[2 * n_cast + 4 + n_zero:])
    tl = h_ref.shape[0]
    l = pl.program_id(1)
    n_l = pl.num_programs(1)

    @pl.when((pl.program_id(0) * n_l + l) % CAST_STEPS == 0)
    def _():
        for src, dst in zip(cast_src, cast_dst):
            dst[...] = src[...].astype(BF16)

    for dst in zero_dst:
        dst[...] = jnp.zeros(dst.shape, dst.dtype)

    @pl.when(l == 0)
    def _():
        s_scr[...] = jnp.zeros_like(s_scr)
        ext_scr[:, 0:CONV_HDR, :] = jnp.zeros((QKV_DIM // HEAD_DIM, CONV_HDR, HEAD_DIM), F32)
        pext_scr[:, 0:POOL_HDR, :] = jnp.zeros((len(POOL_WINDOWS), POOL_HDR, POOL_GROUP_DIM), F32)

    hb = h_ref[...].astype(BF16)

    @pl.when((pl.program_id(0) == 0) & (l == 0))
    def _():
        pw_scr[...] = win_ref[:, POOL_OFF:POOL_OFF + POOL_DIM]
        gw_scr[...] = jnp.concatenate([win_ref[:, GATE_OFF:GATE_OFF + N_GATE],
                                       jnp.zeros((D_MODEL, GATE_PAD - N_GATE), BF16)], axis=-1)

    def project(w_ref, lo, width):
        return jnp.dot(hb, w_ref[:, lo:lo + width], preferred_element_type=F32)

    def conv_slab(t, x_t):
        lanes = slice(t * HEAD_DIM, (t + 1) * HEAD_DIM)
        first = CONV_HDR - (CONV_W - 1)
        ext_scr[t, CONV_HDR:CONV_HDR + tl, :] = x_t
        conv = convw_ref[CONV_W - 1:CONV_W, lanes] * x_t
        for i in range(CONV_W - 1):
            conv = conv + convw_ref[i:i + 1, lanes] * ext_scr[t, first + i:first + i + tl, :]
        cfin_ref[:, lanes] = ext_scr[t, tl + first:tl + CONV_HDR, :]
        ext_scr[t, 0:CONV_HDR, :] = ext_scr[t, tl:tl + CONV_HDR, :]
        act = _silu(conv)
        if t < DN_HEADS:
            act = _l2norm(act, HEAD_DIM ** -0.5)
        elif t < 2 * DN_HEADS:
            act = _l2norm(act)
        qkv_scr[t] = act

    def pool_slab(gi, tok):
        w = POOL_WINDOWS[gi]
        lanes = slice(gi * POOL_GROUP_DIM, (gi + 1) * POOL_GROUP_DIM)
        pext_scr[gi, POOL_HDR:POOL_HDR + tl, :] = tok
        s = tok
        for i in range(1, w):
            s = s + pext_scr[gi, POOL_HDR - i:POOL_HDR - i + tl, :]
        pos = l * tl + lax.broadcasted_iota(jnp.int32, (tl, 1), 0)
        cnt = jnp.minimum(w, pos + 1).astype(F32)
        pfin_ref[:, lanes] = pext_scr[gi, tl + POOL_HDR - POOL_BUF:tl + POOL_HDR, :]
        pext_scr[gi, 0:POOL_HDR, :] = pext_scr[gi, tl:tl + POOL_HDR, :]
        return s / cnt - tok

    group = 2 * HEAD_DIM
    for j in range(QKV_DIM // group):
        y = project(win_ref, j * group, group)
        conv_slab(2 * j, y[:, :HEAD_DIM])
        conv_slab(2 * j + 1, y[:, HEAD_DIM:])
    beta_t, g_t = _gates(project(gw_scr, 0, GATE_PAD), alog_ref[...], dtb_ref[...])
    beta_scr[...] = beta_t
    g_scr[...] = g_t
    d_groups = []
    for j in range(POOL_DIM // group):
        y = project(pw_scr, j * group, group)
        d_groups.append(pool_slab(2 * j, y[:, :POOL_GROUP_DIM]))
        d_groups.append(pool_slab(2 * j + 1, y[:, POOL_GROUP_DIM:]))
    opool_scr[...] = _pool_project(d_groups, poolw_ref, pscale_ref[...])
    z_scr[...] = project(win_ref, Z_OFF, DN_DIM)

    c = CHUNK
    row = lax.broadcasted_iota(jnp.int32, (c, c), 0)
    col = lax.broadcasted_iota(jnp.int32, (c, c), 1)
    incl = row >= col
    strict = row > col
    tri = jnp.where(incl, 1.0, 0.0).astype(BF16)

    n_chunks = tl // c
    chains = [(ci, hd) for ci in range(n_chunks) for hd in range(DN_HEADS)]
    gc_alls, gc_ts, beta_alls = [], [], []
    for ci in range(n_chunks):
        rows = slice(ci * c, (ci + 1) * c)
        gc_all = _dot_exact_lhs(tri, g_scr[rows, :])
        gc_alls.append(gc_all)
        gc_ts.append(gc_all.T)
        beta_alls.append(beta_scr[rows, :])
    qs, ks, betas, egs, decs, rhss, k_ends, g_ends = [], [], [], [], [], [], [], []
    for ci, hd in chains:
        rows = slice(ci * c, (ci + 1) * c)
        sl = slice(hd * HEAD_DIM, (hd + 1) * HEAD_DIM)
        lane = DECAY_LANE + hd
        qh, kh, vh = qkv_scr[hd, rows, :], qkv_scr[DN_HEADS + hd, rows, :], qkv_scr[2 * DN_HEADS + hd, rows, :]
        beta = beta_alls[ci][:, hd:hd + 1]
        gc_col = gc_alls[ci][:, lane:lane + 1]
        gc_row = gc_ts[ci][lane:lane + 1, :]
        gc_last = gc_alls[ci][c - 1:c, lane:lane + 1]
        eg = jnp.exp(gc_col)
        decs.append(jnp.exp(jnp.where(incl, gc_col - gc_row, MASKED_LOG)))
        rhss.append(jnp.concatenate([beta * vh, (beta * eg) * kh], axis=-1))
        k_ends.append(kh * jnp.exp(gc_last - gc_col))
        g_ends.append(jnp.exp(gc_last))
        qs.append(qh); ks.append(kh); betas.append(beta); egs.append(eg)
    kqs = [_dot_nt(jnp.concatenate([kh, qh], axis=0), kh) for kh, qh in zip(ks, qs)]
    ms = [jnp.where(strict, beta * kq[:c] * dec, 0.0) for beta, kq, dec in zip(betas, kqs, decs)]
    qks = [kq[c:] * dec for kq, dec in zip(kqs, decs)]
    t_invs = _inv_unit_lower(ms, row, col, c)
    sols = [_dot(t_inv, rhs) for t_inv, rhs in zip(t_invs, rhss)]
    q_decs = [qh * eg for qh, eg in zip(qs, egs)]

    s_cur = [s_scr[hd] for hd in range(DN_HEADS)]
    o_chunks = []
    for ci in range(n_chunks):
        idx = [ci * DN_HEADS + hd for hd in range(DN_HEADS)]
        wss = [_dot(jnp.concatenate([sols[i][:, HEAD_DIM:], q_decs[i]], axis=0), s) for i, s in zip(idx, s_cur)]
        us = [sols[i][:, :HEAD_DIM] - ws[:c] for i, ws in zip(idx, wss)]
        outs = [ws[c:] + _dot(qks[i], u) for i, ws, u in zip(idx, wss, us)]
        s_cur = [g_ends[i] * s + _dot_tn(k_ends[i], u) for i, s, u in zip(idx, s_cur, us)]
        o_chunks.append(jnp.concatenate(outs, axis=-1))
        if len(o_chunks) == OUT_CHUNKS or ci == n_chunks - 1:
            rows = slice((ci + 1 - len(o_chunks)) * c, (ci + 1) * c)
            o_dn = _gated_out_norm(jnp.concatenate(o_chunks, axis=0), z_scr[rows, :], onorm_ref[...])
            mix = _dot(jnp.concatenate([o_dn, opool_scr[rows, :]], axis=-1), wout_ref[...])
            y_ref[rows, :] = _layer_norm(DN_ALPHA * h_ref[rows, :] + mix, g2_ref[...], b2_ref[...])
            o_chunks = []
    for hd in range(DN_HEADS):
        s_scr[hd] = s_cur[hd]

    @pl.when(l == n_l - 1)
    def _():
        sfin_ref[...] = s_scr[...]


def _mixer_weight_specs(layer):
    return [
        _layer_resident(layer, (D_MODEL, IN_DIM)),
        _layer_resident(layer, (CONV_W, QKV_DIM)),
        _layer_resident(layer, (1, GATE_PAD)),
        _layer_resident(layer, (1, GATE_PAD)),
        _layer_resident(layer, (1, HEAD_DIM)),
        _layer_resident(layer, (len(POOL_WINDOWS), POOL_GROUP_DIM, POOL_GROUP_DIM)),
        _layer_resident(layer, (1, POOL_DIM)),
        _layer_resident(layer, (DN_DIM + POOL_DIM, D_MODEL)),
        _layer_resident(layer, (1, D_MODEL)),
        _layer_resident(layer, (1, D_MODEL)),
    ]


def _mixer_prompt(layer, h, weights, cast_jobs, zero_shapes=()):
    bsz, seq, _ = h.shape
    tl = MIX_ROWS
    n_l = seq // tl
    assert seq % tl == 0 and tl % CHUNK == 0 and (bsz * n_l) % CAST_STEPS == 0
    n_blocks = bsz * n_l // CAST_STEPS
    out_shape = [
        jax.ShapeDtypeStruct((bsz, seq, D_MODEL), F32),
        jax.ShapeDtypeStruct((bsz, DN_HEADS, HEAD_DIM, HEAD_DIM), F32),
        jax.ShapeDtypeStruct((bsz, CONV_W - 1, QKV_DIM), F32),
        jax.ShapeDtypeStruct((bsz, POOL_BUF, POOL_DIM), F32),
    ]
    out_specs = [
        pl.BlockSpec((None, tl, D_MODEL), lambda b, l: (b, l, 0)),
        pl.BlockSpec((None, DN_HEADS, HEAD_DIM, HEAD_DIM), lambda b, l: (b, 0, 0, 0)),
        pl.BlockSpec((None, CONV_W - 1, QKV_DIM), lambda b, l: (b, 0, 0)),
        pl.BlockSpec((None, POOL_BUF, POOL_DIM), lambda b, l: (b, 0, 0)),
    ]
    cast_specs = []
    for w_all, w_layer in cast_jobs:
        _, n_rows, n_cols = w_all.shape
        assert n_rows % (n_blocks * BF16_SUBLANES) == 0
        blk = n_rows // n_blocks
        cast_specs.append(pl.BlockSpec((None, blk, n_cols),
                                       lambda b, l, w_layer=w_layer: (w_layer, (b * n_l + l) // CAST_STEPS, 0)))
        out_shape.append(jax.ShapeDtypeStruct((n_rows, n_cols), BF16))
        out_specs.append(pl.BlockSpec((blk, n_cols), lambda b, l: ((b * n_l + l) // CAST_STEPS, 0)))
    for shape in zero_shapes:
        assert shape[0] % (bsz * n_l) == 0
        rest = tuple(shape[1:])
        out_shape.append(jax.ShapeDtypeStruct(tuple(shape), F32))
        out_specs.append(pl.BlockSpec((shape[0] // (bsz * n_l),) + rest,
                                      lambda b, l, nd=len(rest): (b * n_l + l,) + (0,) * nd))
    return pl.pallas_call(
        functools.partial(_mixer_prompt_kernel, len(cast_jobs), len(zero_shapes)),
        out_shape=tuple(out_shape),
        grid=(bsz, n_l),
        in_specs=[
            pl.BlockSpec((None, tl, D_MODEL), lambda b, l: (b, l, 0)),
        ] + _mixer_weight_specs(layer) + cast_specs,
        out_specs=tuple(out_specs),
        scratch_shapes=[
            pltpu.VMEM((DN_HEADS, HEAD_DIM, HEAD_DIM), F32),
            pltpu.VMEM((QKV_DIM // HEAD_DIM, tl + CONV_HDR, HEAD_DIM), F32),
            pltpu.VMEM((len(POOL_WINDOWS), tl + POOL_HDR, POOL_GROUP_DIM), F32),
            pltpu.VMEM((QKV_DIM // HEAD_DIM, tl, HEAD_DIM), F32),
            pltpu.VMEM((tl, GATE_PAD), F32),
            pltpu.VMEM((tl, GATE_PAD), F32),
            pltpu.VMEM((tl, DN_DIM), F32),
            pltpu.VMEM((tl, POOL_DIM), F32),
            pltpu.VMEM((D_MODEL, POOL_DIM), BF16),
            pltpu.VMEM((D_MODEL, GATE_PAD), BF16),
        ],
        compiler_params=pltpu.CompilerParams(dimension_semantics=("arbitrary", "arbitrary")),
        name="mixer_prompt",
    )(h, *weights, *[w_all for w_all, _ in cast_jobs])


N_SAMPLE_STATES = 3

def _mixer_sample_kernel(h_ref, s_ref, cbuf_ref, pbuf_ref, win_ref, convw_ref, alog_ref, dtb_ref, onorm_ref,
                         poolw_ref, pscale_ref, wout_ref, g2_ref, b2_ref, *refs):
    refs = refs[N_SAMPLE_STATES:]
    y_ref, snew_ref, cnew_ref, pnew_ref = refs[:4]
    q_scr, k_scr, v_scr, beta_scr, eg_scr, z_scr, opool_scr, o_scr = refs[4:]
    i = pl.program_id(0)
    n_i = pl.num_programs(0)

    @pl.when(i == 0)
    def _():
        proj = jnp.dot(h_ref[...].astype(BF16), win_ref[...], preferred_element_type=F32)
        qkv = proj[:, 0:QKV_DIM]
        conv = convw_ref[CONV_W - 1:CONV_W, :] * qkv
        for j in range(CONV_W - 1):
            conv = conv + convw_ref[j:j + 1, :] * cbuf_ref[j]
        for j in range(CONV_W - 2):
            cnew_ref[j] = cbuf_ref[j + 1]
        cnew_ref[CONV_W - 2] = qkv
        q, k, v = _qkv_heads(conv)
        q_scr[...] = q
        k_scr[...] = k
        v_scr[...] = v
        ba = jnp.concatenate([proj[:, GATE_OFF:GATE_OFF + N_GATE],
                              jnp.zeros((proj.shape[0], GATE_PAD - N_GATE), F32)], axis=-1)
        beta_t, g_t = _gates(ba, alog_ref[...], dtb_ref[...])
        beta_scr[...] = beta_t
        eg_scr[...] = jnp.exp(g_t)
        z_scr[...] = proj[:, Z_OFF:Z_OFF + DN_DIM]

        p = proj[:, POOL_OFF:POOL_OFF + POOL_DIM]
        d_groups = []
        for gi, w in enumerate(POOL_WINDOWS):
            lo = gi * POOL_GROUP_DIM
            tok = p[:, lo:lo + POOL_GROUP_DIM]
            s = tok
            for back in range(1, w):
                s = s + pbuf_ref[POOL_BUF - back, :, lo:lo + POOL_GROUP_DIM]
            cnt = float(min(w, PAST_LEN + 1))
            d_groups.append(s / cnt - tok)
        opool_scr[...] = _pool_project(d_groups, poolw_ref, pscale_ref[...])
        for j in range(POOL_BUF - 1):
            pnew_ref[j] = pbuf_ref[j + 1]
        pnew_ref[POOL_BUF - 1] = p

    rows = pl.ds(pl.multiple_of(i * SAMPLE_TILE, SAMPLE_TILE), SAMPLE_TILE)
    q_t, k_t, v_t = q_scr[rows, :], k_scr[rows, :], v_scr[rows, :]
    beta_t, a_t = beta_scr[rows, :], eg_scr[rows, :]
    eye = jnp.where(lax.broadcasted_iota(jnp.int32, (HEAD_DIM, HEAD_DIM), 0)
                    == lax.broadcasted_iota(jnp.int32, (HEAD_DIM, HEAD_DIM), 1), 1.0, 0.0).astype(F32)
    pad_rows = jnp.zeros((F32_SUBLANES - 2, HEAD_DIM), F32)
    o_heads = [[] for _ in range(SAMPLE_TILE)]
    for hd in range(DN_HEADS):
        sl = slice(hd * HEAD_DIM, (hd + 1) * HEAD_DIM)
        for j in range(SAMPLE_TILE):
            qh = q_t[j:j + 1, sl]
            kh = k_t[j:j + 1, sl]
            vh = v_t[j:j + 1, sl]
            beta = beta_t[j:j + 1, hd:hd + 1]
            a = a_t[j:j + 1, DECAY_LANE + hd:DECAY_LANE + hd + 1]
            s_old = s_ref[j, hd]
            kq = jnp.concatenate([kh, qh, pad_rows], axis=0)
            kq_hi, kq_lo = _split_bf16(kq)
            s_hi, s_lo = _split_bf16(s_old)
            mv = functools.partial(jnp.dot, preferred_element_type=F32)
            kqs = mv(kq_hi, s_hi) + (mv(kq_hi, s_lo) + mv(kq_lo, s_hi))
            ks, qs = kqs[0:1], kqs[1:2]
            k_col = jnp.sum(eye * kh, axis=-1, keepdims=True)
            u = beta * vh - (beta * a) * ks
            qk = jnp.sum(qh * kh, axis=-1, keepdims=True)
            o_heads[j].append(a * qs + qk * u)
            snew_ref[j, hd] = a * s_old + k_col * u
    o_scr[rows, :] = jnp.concatenate([jnp.concatenate(o_h, axis=-1) for o_h in o_heads], axis=0)

    @pl.when(i == n_i - 1)
    def _():
        o_dn = _gated_out_norm(o_scr[...], z_scr[...], onorm_ref[...])
        mix = _dot(jnp.concatenate([o_dn, opool_scr[...]], axis=-1), wout_ref[...])
        y_ref[...] = _layer_norm(DN_ALPHA * h_ref[...] + mix, g2_ref[...], b2_ref[...])


def _mixer_sample(layer, h, s_all, cbuf_all, pbuf_all, new_states, *weights):
    bsz = h.shape[0]
    assert bsz % SAMPLE_TILE == 0 and len(new_states) == N_SAMPLE_STATES
    cshape = (CONV_W - 1, bsz, QKV_DIM)
    pshape = (POOL_BUF, bsz, POOL_DIM)
    assert cbuf_all.shape[1:] == cshape and pbuf_all.shape[1:] == pshape
    out_shape = (
        jax.ShapeDtypeStruct((bsz, D_MODEL), F32),
        jax.ShapeDtypeStruct(s_all.shape, F32),
        jax.ShapeDtypeStruct(cbuf_all.shape, F32),
        jax.ShapeDtypeStruct(pbuf_all.shape, F32),
    )
    state_spec = pl.BlockSpec((None, SAMPLE_TILE, DN_HEADS, HEAD_DIM, HEAD_DIM), lambda i: (layer, i, 0, 0, 0))
    n_in = 4 + 10
    return pl.pallas_call(
        _mixer_sample_kernel,
        out_shape=out_shape,
        grid=(bsz // SAMPLE_TILE,),
        in_specs=[
            _resident((bsz, D_MODEL)),
            state_spec,
            _layer_resident(layer, cshape),
            _layer_resident(layer, pshape),
        ] + _mixer_weight_specs(layer) + [pl.BlockSpec(memory_space=pl.ANY)] * len(new_states),
        out_specs=(
            pl.BlockSpec((bsz, D_MODEL), lambda i: (0, 0)),
            state_spec,
            pl.BlockSpec((None,) + cshape, lambda i: (layer, 0, 0, 0)),
            pl.BlockSpec((None,) + pshape, lambda i: (layer, 0, 0, 0)),
        ),
        input_output_aliases={n_in + k: 1 + k for k in range(len(new_states))},
        scratch_shapes=[
            pltpu.VMEM((bsz, DN_DIM), F32),
            pltpu.VMEM((bsz, DN_DIM), F32),
            pltpu.VMEM((bsz, DN_DIM), F32),
            pltpu.VMEM((bsz, GATE_PAD), F32),
            pltpu.VMEM((bsz, GATE_PAD), F32),
            pltpu.VMEM((bsz, DN_DIM), F32),
            pltpu.VMEM((bsz, POOL_DIM), F32),
            pltpu.VMEM((bsz, DN_DIM), F32),
        ],
        compiler_params=pltpu.CompilerParams(dimension_semantics=("arbitrary",)),
        name="mixer_sample",
    )(h, s_all, cbuf_all, pbuf_all, *weights, *new_states)


def _pad_gate_param(v):
    out = jnp.zeros((v.shape[0], 1, GATE_PAD), F32)
    return out.at[:, 0, DECAY_LANE:DECAY_LANE + DN_HEADS].set(v.astype(F32))


def kernel(x_prompt, x_sample, state_delta, state_conv, state_pool, ln1_g, ln1_b, ffn1_w_gate, ffn1_w_up, ffn1_w_down, w_in, conv_w, a_log, dt_bias, onorm_g, pool_w, pool_scale, w_out, ln2_g, ln2_b, ffn2_w_gate, ffn2_w_up, ffn2_w_down, ln3_g, ln3_b):
    bsz, seq, _ = x_prompt.shape
    dec_b = x_sample.shape[0]

    win = w_in.astype(BF16)
    wout = w_out.astype(BF16)
    poolw = pool_w.astype(BF16)
    ffn1_stacks = (ffn1_w_gate, ffn1_w_up, ffn1_w_down)
    ffn2_stacks = (ffn2_w_gate, ffn2_w_up, ffn2_w_down)
    f1 = tuple(w[0].astype(BF16) for w in ffn1_stacks)
    alog = _pad_gate_param(a_log)
    dtb = _pad_gate_param(dt_bias)
    row = lambda v: v.reshape(DEPTH, 1, -1)
    ln1g, ln1b, ln2g, ln2b, ln3g, ln3b = map(row, (ln1_g, ln1_b, ln2_g, ln2_b, ln3_g, ln3_b))
    onorm = row(onorm_g)
    pscale = row(pool_scale)
    mix_w = (win, conv_w, alog, dtb, onorm, poolw, pscale, wout, ln2g, ln2b)

    xp = x_prompt.reshape(bsz * seq, D_MODEL)
    xs = x_sample.reshape(dec_b, D_MODEL)
    dp, cp, pp = [], [], []
    cbuf_all = jnp.transpose(state_conv, (0, 2, 1, 3))
    pbuf_all = jnp.transpose(state_pool, (0, 2, 1, 3))
    state_shapes = [state_delta.shape, cbuf_all.shape, pbuf_all.shape]
    flat_shapes = [(DEPTH * dec_b,) + state_delta.shape[2:],
                   (DEPTH * (CONV_W - 1) * dec_b, QKV_DIM), (DEPTH * POOL_BUF * dec_b, POOL_DIM)]
    for l in range(DEPTH):
        hp, hs = _ffn_ln(l, xp, xs, *f1, ln1g, ln1b)
        cast_jobs = [(w, l) for w in ffn2_stacks]
        if l + 1 < DEPTH:
            cast_jobs += [(w, l + 1) for w in ffn1_stacks]
        hp, s_p, c_p, p_p, *extra = _mixer_prompt(l, hp.reshape(bsz, seq, D_MODEL), mix_w, cast_jobs,
                                                  flat_shapes if l == 0 else ())
        cast = extra[:len(cast_jobs)]
        if l == 0:
            sample_states = tuple(z.reshape(s) for z, s in zip(extra[len(cast_jobs):], state_shapes))
        f2, f1 = tuple(cast[:3]), tuple(cast[3:])
        hs, *sample_states = _mixer_sample(l, hs, state_delta, cbuf_all, pbuf_all, tuple(sample_states), *mix_w)
        xp, xs = _ffn_ln(l, hp.reshape(bsz * seq, D_MODEL), hs, *f2, ln3g, ln3b)
        dp.append(s_p); cp.append(c_p); pp.append(p_p)
    delta_sample, conv_sample, pool_sample = sample_states
    return (xp.reshape(bsz, seq, D_MODEL), xs.reshape(dec_b, 1, D_MODEL),
            jnp.stack(dp), jnp.stack(cp), jnp.stack(pp),
            delta_sample, jnp.transpose(conv_sample, (0, 2, 1, 3)), jnp.transpose(pool_sample, (0, 2, 1, 3)))
```

```python
import functools

import jax
import jax.numpy as jnp
from jax import lax
from jax.experimental import pallas as pl
from jax.experimental.pallas import tpu as pltpu

F32 = jnp.float32
BF16 = jnp.bfloat16

D_MODEL = 1024
DEPTH = 4
DN_HEADS = 4
HEAD_DIM = 128
DN_DIM = DN_HEADS * HEAD_DIM
POOL_DIM = 512
QKV_DIM = 3 * DN_DIM
CONV_W = 4
POOL_WINDOWS = (2, 4, 8, 16)
POOL_GROUP_DIM = POOL_DIM // len(POOL_WINDOWS)
POOL_BUF = max(POOL_WINDOWS) - 1
D_FF = 2816
PAST_LEN = 16384
DN_ALPHA = (2.0 * DEPTH) ** 0.25
LN_EPS = 1e-5
RMS_EPS = 1e-6
L2_EPS = 1e-6

N_GATE = 2 * DN_HEADS
Z_OFF = QKV_DIM
GATE_OFF = QKV_DIM + DN_DIM
POOL_OFF = GATE_OFF + N_GATE
IN_DIM = POOL_OFF + POOL_DIM
GATE_PAD = 128
DECAY_LANE = DN_HEADS

CHUNK = 128
INV_BASE = 8
FFN_ROWS = 1024
FFN_SUB_ROWS = 256
MIX_ROWS = 512
OUT_CHUNKS = 2
CAST_STEPS = 2
BF16_SUBLANES = 16
F32_SUBLANES = 8
CONV_HDR = F32_SUBLANES
POOL_HDR = -(-POOL_BUF // F32_SUBLANES) * F32_SUBLANES
MASKED_LOG = -1e30
SAMPLE_TILE = 16


def _dot(a, b):
    return jnp.dot(a.astype(BF16), b.astype(BF16), preferred_element_type=F32)


def _dot_nt(a, b):
    return lax.dot_general(a.astype(BF16), b.astype(BF16), (((1,), (1,)), ((), ())),
                           preferred_element_type=F32)


def _dot_tn(a, b):
    return lax.dot_general(a.astype(BF16), b.astype(BF16), (((0,), (0,)), ((), ())),
                           preferred_element_type=F32)


def _split_bf16(x):
    hi = x.astype(BF16)
    lo = (x - hi.astype(F32)).astype(BF16)
    return hi, lo


def _dot_exact_lhs(a_bf16, b):
    b1 = b.astype(BF16)
    r1 = b - b1.astype(F32)
    b2 = r1.astype(BF16)
    b3 = (r1 - b2.astype(F32)).astype(BF16)
    d = functools.partial(jnp.dot, preferred_element_type=F32)
    return d(a_bf16, b1) + (d(a_bf16, b2) + d(a_bf16, b3))


def _silu(x):
    return x * jax.nn.sigmoid(x)


def _softplus(x):
    return jnp.maximum(x, 0.0) + jnp.log1p(jnp.exp(-jnp.abs(x)))


def _layer_norm(y, g, b):
    mu = jnp.mean(y, axis=-1, keepdims=True)
    yc = y - mu
    var = jnp.mean(yc * yc, axis=-1, keepdims=True)
    return yc * lax.rsqrt(var + LN_EPS) * g + b


def _l2norm(x, scale=None):
    inv = lax.rsqrt(jnp.sum(x * x, axis=-1, keepdims=True) + L2_EPS)
    return x * (inv if scale is None else inv * scale)


def _ffn_ln_rows(x, wg_ref, wu_ref, wd_ref, g_ref, b_ref):
    xb = x.astype(BF16)
    gate = jnp.dot(xb, wg_ref[...], preferred_element_type=F32)
    up = jnp.dot(xb, wu_ref[...], preferred_element_type=F32)
    act = (_silu(gate) * up).astype(BF16)
    ff = jnp.dot(act, wd_ref[...], preferred_element_type=F32)
    return _layer_norm(DN_ALPHA * x + 0.5 * ff, g_ref[...], b_ref[...])


def _ffn_ln_kernel(x_ref, xs_ref, wg_ref, wu_ref, wd_ref, g_ref, b_ref, o_ref, os_ref):
    weights = (wg_ref, wu_ref, wd_ref, g_ref, b_ref)
    tm = x_ref.shape[0]
    for r in range(tm // FFN_SUB_ROWS):
        rows = slice(r * FFN_SUB_ROWS, (r + 1) * FFN_SUB_ROWS)
        o_ref[rows, :] = _ffn_ln_rows(x_ref[rows, :], *weights)

    @pl.when(pl.program_id(0) == pl.num_programs(0) - 1)
    def _():
        os_ref[...] = _ffn_ln_rows(xs_ref[...], *weights)


def _resident(shape):
    zeros = (0,) * len(shape)
    return pl.BlockSpec(shape, lambda *_: zeros, pipeline_mode=pl.Buffered(1))


def _layer_resident(layer, shape):
    index = (layer,) + (0,) * len(shape)
    return pl.BlockSpec((None,) + tuple(shape), lambda *_: index, pipeline_mode=pl.Buffered(1))


def _ffn_ln(layer, x, xs, wg, wu, wd, g, b):
    rows = x.shape[0]
    tm = FFN_ROWS
    assert rows % tm == 0 and tm % FFN_SUB_ROWS == 0
    return pl.pallas_call(
        _ffn_ln_kernel,
        out_shape=(jax.ShapeDtypeStruct((rows, D_MODEL), F32), jax.ShapeDtypeStruct(xs.shape, F32)),
        grid=(rows // tm,),
        in_specs=[
            pl.BlockSpec((tm, D_MODEL), lambda i: (i, 0)),
            _resident(xs.shape),
            _resident((D_MODEL, D_FF)),
            _resident((D_MODEL, D_FF)),
            _resident((D_FF, D_MODEL)),
            _layer_resident(layer, (1, D_MODEL)),
            _layer_resident(layer, (1, D_MODEL)),
        ],
        out_specs=(pl.BlockSpec((tm, D_MODEL), lambda i: (i, 0)),
                   pl.BlockSpec(xs.shape, lambda i: (0, 0))),
        compiler_params=pltpu.CompilerParams(dimension_semantics=("arbitrary",)),
        name="ffn_ln",
    )(x, xs, wg, wu, wd, g, b)


def _gates(ba, alog, dtb):
    beta = jax.nn.sigmoid(ba)
    g = -jnp.exp(alog) * _softplus(ba + dtb)
    return beta, g


def _qkv_heads(conv_out):
    act = _silu(conv_out)
    qs, ks = [], []
    for h in range(DN_HEADS):
        lo = h * HEAD_DIM
        qs.append(_l2norm(act[:, lo:lo + HEAD_DIM], HEAD_DIM ** -0.5))
        ks.append(_l2norm(act[:, DN_DIM + lo:DN_DIM + lo + HEAD_DIM]))
    return jnp.concatenate(qs, -1), jnp.concatenate(ks, -1), act[:, 2 * DN_DIM:]


def _gated_out_norm(o, z, onorm_g):
    outs = []
    for h in range(DN_HEADS):
        sl = slice(h * HEAD_DIM, (h + 1) * HEAD_DIM)
        oh = o[:, sl]
        oh = oh * lax.rsqrt(jnp.mean(oh * oh, axis=-1, keepdims=True) + RMS_EPS)
        outs.append(oh * onorm_g * _silu(z[:, sl]))
    return jnp.concatenate(outs, -1)


def _pool_project(d_groups, poolw_ref, pscale):
    ys = [_dot(d, poolw_ref[gi]) for gi, d in enumerate(d_groups)]
    return jnp.concatenate(ys, -1) * pscale


def _inv_unit_lower(ms, row, col, size):
    eye = jnp.where(row == col, 1.0, 0.0).astype(F32)
    base_bits = INV_BASE.bit_length() - 1
    base_mask = (row >> base_bits) == (col >> base_bits)
    ps = [jnp.where(base_mask, m, 0.0) for m in ms]
    xs = [eye - p for p in ps]
    ps = [_dot(p, p) for p in ps]
    for step in range(base_bits - 1):
        if step < base_bits - 2:
            xps = [_dot(jnp.concatenate([x, p], axis=0), p) for x, p in zip(xs, ps)]
            xs = [x + xp[:size] for x, xp in zip(xs, xps)]
            ps = [xp[size:] for xp in xps]
        else:
            xs = [x + _dot(x, p) for x, p in zip(xs, ps)]
    bits = base_bits
    while (1 << bits) < size:
        same_pair = (row >> (bits + 1)) == (col >> (bits + 1))
        lower_left = (((row >> bits) & 1) == 1) & (((col >> bits) & 1) == 0)
        mask = same_pair & lower_left
        ys = [_dot(jnp.where(mask, m, 0.0), x) for m, x in zip(ms, xs)]
        xs = [x - _dot(x, y) for x, y in zip(xs, ys)]
        bits += 1
    return xs


def _mixer_prompt_kernel(n_cast, n_zero, h_ref, win_ref, convw_ref, alog_ref, dtb_ref, onorm_ref, poolw_ref,
                         pscale_ref, wout_ref, g2_ref, b2_ref, *refs):
    cast_src = refs[:n_cast]
    y_ref, sfin_ref, cfin_ref, pfin_ref = refs[n_cast:n_cast + 4]
    cast_dst = refs[n_cast + 4:2 * n_cast + 4]
    zero_dst = refs[2 * n_cast + 4:2 * n_cast + 4 + n_zero]
    s_scr, ext_scr, pext_scr, qkv_scr, beta_scr, g_scr, z_scr, opool_scr, pw_scr, gw_scr = (
        refs[2 * n_cast + 4 + n_zero:])
    tl = h_ref.shape[0]
    l = pl.program_id(1)
    n_l = pl.num_programs(1)

    @pl.when((pl.program_id(0) * n_l + l) % CAST_STEPS == 0)
    def _():
        for src, dst in zip(cast_src, cast_dst):
            dst[...] = src[...].astype(BF16)

    for dst in zero_dst:
        dst[...] = jnp.zeros(dst.shape, dst.dtype)

    @pl.when(l == 0)
    def _():
        s_scr[...] = jnp.zeros_like(s_scr)
        ext_scr[:, 0:CONV_HDR, :] = jnp.zeros((QKV_DIM // HEAD_DIM, CONV_HDR, HEAD_DIM), F32)
        pext_scr[:, 0:POOL_HDR, :] = jnp.zeros((len(POOL_WINDOWS), POOL_HDR, POOL_GROUP_DIM), F32)

    hb = h_ref[...].astype(BF16)

    @pl.when((pl.program_id(0) == 0) & (l == 0))
    def _():
        pw_scr[...] = win_ref[:, POOL_OFF:POOL_OFF + POOL_DIM]
        gw_scr[...] = jnp.concatenate([win_ref[:, GATE_OFF:GATE_OFF + N_GATE],
                                       jnp.zeros((D_MODEL, GATE_PAD - N_GATE), BF16)], axis=-1)

    def project(w_ref, lo, width):
        return jnp.dot(hb, w_ref[:, lo:lo + width], preferred_element_type=F32)

    def conv_slab(t, x_t):
        lanes = slice(t * HEAD_DIM, (t + 1) * HEAD_DIM)
        first = CONV_HDR - (CONV_W - 1)
        ext_scr[t, CONV_HDR:CONV_HDR + tl, :] = x_t
        conv = convw_ref[CONV_W - 1:CONV_W, lanes] * x_t
        for i in range(CONV_W - 1):
            conv = conv + convw_ref[i:i + 1, lanes] * ext_scr[t, first + i:first + i + tl, :]
        cfin_ref[:, lanes] = ext_scr[t, tl + first:tl + CONV_HDR, :]
        ext_scr[t, 0:CONV_HDR, :] = ext_scr[t, tl:tl + CONV_HDR, :]
        act = _silu(conv)
        if t < DN_HEADS:
            act = _l2norm(act, HEAD_DIM ** -0.5)
        elif t < 2 * DN_HEADS:
            act = _l2norm(act)
        qkv_scr[t] = act

    def pool_slab(gi, tok):
        w = POOL_WINDOWS[gi]
        lanes = slice(gi * POOL_GROUP_DIM, (gi + 1) * POOL_GROUP_DIM)
        pext_scr[gi, POOL_HDR:POOL_HDR + tl, :] = tok
        s = tok
        for i in range(1, w):
            s = s + pext_scr[gi, POOL_HDR - i:POOL_HDR - i + tl, :]
        pos = l * tl + lax.broadcasted_iota(jnp.int32, (tl, 1), 0)
        cnt = jnp.minimum(w, pos + 1).astype(F32)
        pfin_ref[:, lanes] = pext_scr[gi, tl + POOL_HDR - POOL_BUF:tl + POOL_HDR, :]
        pext_scr[gi, 0:POOL_HDR, :] = pext_scr[gi, tl:tl + POOL_HDR, :]
        return s / cnt - tok

    group = 2 * HEAD_DIM
    for j in range(QKV_DIM // group):
        y = project(win_ref, j * group, group)
        conv_slab(2 * j, y[:, :HEAD_DIM])
        conv_slab(2 * j + 1, y[:, HEAD_DIM:])
    beta_t, g_t = _gates(project(gw_scr, 0, GATE_PAD), alog_ref[...], dtb_ref[...])
    beta_scr[...] = beta_t
    g_scr[...] = g_t
    d_groups = []
    for j in range(POOL_DIM // group):
        y = project(pw_scr, j * group, group)
        d_groups.append(pool_slab(2 * j, y[:, :POOL_GROUP_DIM]))
        d_groups.append(pool_slab(2 * j + 1, y[:, POOL_GROUP_DIM:]))
    opool_scr[...] = _pool_project(d_groups, poolw_ref, pscale_ref[...])
    z_scr[...] = project(win_ref, Z_OFF, DN_DIM)

    c = CHUNK
    row = lax.broadcasted_iota(jnp.int32, (c, c), 0)
    col = lax.broadcasted_iota(jnp.int32, (c, c), 1)
    incl = row >= col
    strict = row > col
    tri = jnp.where(incl, 1.0, 0.0).astype(BF16)

    n_chunks = tl // c
    chains = [(ci, hd) for ci in range(n_chunks) for hd in range(DN_HEADS)]
    gc_alls, gc_ts, beta_alls = [], [], []
    for ci in range(n_chunks):
        rows = slice(ci * c, (ci + 1) * c)
        gc_all = _dot_exact_lhs(tri, g_scr[rows, :])
        gc_alls.append(gc_all)
        gc_ts.append(gc_all.T)
        beta_alls.append(beta_scr[rows, :])
    qs, ks, betas, egs, decs, rhss, k_ends, g_ends = [], [], [], [], [], [], [], []
    for ci, hd in chains:
        rows = slice(ci * c, (ci + 1) * c)
        sl = slice(hd * HEAD_DIM, (hd + 1) * HEAD_DIM)
        lane = DECAY_LANE + hd
        qh, kh, vh = qkv_scr[hd, rows, :], qkv_scr[DN_HEADS + hd, rows, :], qkv_scr[2 * DN_HEADS + hd, rows, :]
        beta = beta_alls[ci][:, hd:hd + 1]
        gc_col = gc_alls[ci][:, lane:lane + 1]
        gc_row = gc_ts[ci][lane:lane + 1, :]
        gc_last = gc_alls[ci][c - 1:c, lane:lane + 1]
        eg = jnp.exp(gc_col)
        decs.append(jnp.exp(jnp.where(incl, gc_col - gc_row, MASKED_LOG)))
        rhss.append(jnp.concatenate([beta * vh, (beta * eg) * kh], axis=-1))
        k_ends.append(kh * jnp.exp(gc_last - gc_col))
        g_ends.append(jnp.exp(gc_last))
        qs.append(qh); ks.append(kh); betas.append(beta); egs.append(eg)
    kqs = [_dot(jnp.concatenate([kh, qh], axis=0), kh.T) for kh, qh in zip(ks, qs)]
    ms = [jnp.where(strict, beta * kq[:c] * dec, 0.0) for beta, kq, dec in zip(betas, kqs, decs)]
    qks = [kq[c:] * dec for kq, dec in zip(kqs, decs)]
    t_invs = _inv_unit_lower(ms, row, col, c)
    sols = [_dot(t_inv, rhs) for t_inv, rhs in zip(t_invs, rhss)]
    q_decs = [qh * eg for qh, eg in zip(qs, egs)]

    s_cur = [s_scr[hd] for hd in range(DN_HEADS)]
    o_chunks = []
    for ci in range(n_chunks):
        idx = [ci * DN_HEADS + hd for hd in range(DN_HEADS)]
        wss = [_dot(jnp.concatenate([sols[i][:, HEAD_DIM:], q_decs[i]], axis=0), s) for i, s in zip(idx, s_cur)]
        us = [sols[i][:, :HEAD_DIM] - ws[:c] for i, ws in zip(idx, wss)]
        outs = [ws[c:] + _dot(qks[i], u) for i, ws, u in zip(idx, wss, us)]
        s_cur = [g_ends[i] * s + _dot_tn(k_ends[i], u) for i, s, u in zip(idx, s_cur, us)]
        o_chunks.append(jnp.concatenate(outs, axis=-1))
        if len(o_chunks) == OUT_CHUNKS or ci == n_chunks - 1:
            rows = slice((ci + 1 - len(o_chunks)) * c, (ci + 1) * c)
            o_dn = _gated_out_norm(jnp.concatenate(o_chunks, axis=0), z_scr[rows, :], onorm_ref[...])
            mix = _dot(jnp.concatenate([o_dn, opool_scr[rows, :]], axis=-1), wout_ref[...])
            y_ref[rows, :] = _layer_norm(DN_ALPHA * h_ref[rows, :] + mix, g2_ref[...], b2_ref[...])
            o_chunks = []
    for hd in range(DN_HEADS):
        s_scr[hd] = s_cur[hd]

    @pl.when(l == n_l - 1)
    def _():
        sfin_ref[...] = s_scr[...]


def _mixer_weight_specs(layer):
    return [
        _layer_resident(layer, (D_MODEL, IN_DIM)),
        _layer_resident(layer, (CONV_W, QKV_DIM)),
        _layer_resident(layer, (1, GATE_PAD)),
        _layer_resident(layer, (1, GATE_PAD)),
        _layer_resident(layer, (1, HEAD_DIM)),
        _layer_resident(layer, (len(POOL_WINDOWS), POOL_GROUP_DIM, POOL_GROUP_DIM)),
        _layer_resident(layer, (1, POOL_DIM)),
        _layer_resident(layer, (DN_DIM + POOL_DIM, D_MODEL)),
        _layer_resident(layer, (1, D_MODEL)),
        _layer_resident(layer, (1, D_MODEL)),
    ]


def _mixer_prompt(layer, h, weights, cast_jobs, zero_shapes=()):
    bsz, seq, _ = h.shape
    tl = MIX_ROWS
    n_l = seq // tl
    assert seq % tl == 0 and tl % CHUNK == 0 and (bsz * n_l) % CAST_STEPS == 0
    n_blocks = bsz * n_l // CAST_STEPS
    out_shape = [
        jax.ShapeDtypeStruct((bsz, seq, D_MODEL), F32),
        jax.ShapeDtypeStruct((bsz, DN_HEADS, HEAD_DIM, HEAD_DIM), F32),
        jax.ShapeDtypeStruct((bsz, CONV_W - 1, QKV_DIM), F32),
        jax.ShapeDtypeStruct((bsz, POOL_BUF, POOL_DIM), F32),
    ]
    out_specs = [
        pl.BlockSpec((None, tl, D_MODEL), lambda b, l: (b, l, 0)),
        pl.BlockSpec((None, DN_HEADS, HEAD_DIM, HEAD_DIM), lambda b, l: (b, 0, 0, 0)),
        pl.BlockSpec((None, CONV_W - 1, QKV_DIM), lambda b, l: (b, 0, 0)),
        pl.BlockSpec((None, POOL_BUF, POOL_DIM), lambda b, l: (b, 0, 0)),
    ]
    cast_specs = []
    for w_all, w_layer in cast_jobs:
        _, n_rows, n_cols = w_all.shape
        assert n_rows % (n_blocks * BF16_SUBLANES) == 0
        blk = n_rows // n_blocks
        cast_specs.append(pl.BlockSpec((None, blk, n_cols),
                                       lambda b, l, w_layer=w_layer: (w_layer, (b * n_l + l) // CAST_STEPS, 0)))
        out_shape.append(jax.ShapeDtypeStruct((n_rows, n_cols), BF16))
        out_specs.append(pl.BlockSpec((blk, n_cols), lambda b, l: ((b * n_l + l) // CAST_STEPS, 0)))
    for shape in zero_shapes:
        assert shape[0] % (bsz * n_l) == 0
        rest = tuple(shape[1:])
        out_shape.append(jax.ShapeDtypeStruct(tuple(shape), F32))
        out_specs.append(pl.BlockSpec((shape[0] // (bsz * n_l),) + rest,
                                      lambda b, l, nd=len(rest): (b * n_l + l,) + (0,) * nd))
    return pl.pallas_call(
        functools.partial(_mixer_prompt_kernel, len(cast_jobs), len(zero_shapes)),
        out_shape=tuple(out_shape),
        grid=(bsz, n_l),
        in_specs=[
            pl.BlockSpec((None, tl, D_MODEL), lambda b, l: (b, l, 0)),
        ] + _mixer_weight_specs(layer) + cast_specs,
        out_specs=tuple(out_specs),
        scratch_shapes=[
            pltpu.VMEM((DN_HEADS, HEAD_DIM, HEAD_DIM), F32),
            pltpu.VMEM((QKV_DIM // HEAD_DIM, tl + CONV_HDR, HEAD_DIM), F32),
            pltpu.VMEM((len(POOL_WINDOWS), tl + POOL_HDR, POOL_GROUP_DIM), F32),
            pltpu.VMEM((QKV_DIM // HEAD_DIM, tl, HEAD_DIM), F32),
            pltpu.VMEM((tl, GATE_PAD), F32),
            pltpu.VMEM((tl, GATE_PAD), F32),
            pltpu.VMEM((tl, DN_DIM), F32),
            pltpu.VMEM((tl, POOL_DIM), F32),
            pltpu.VMEM((D_MODEL, POOL_DIM), BF16),
            pltpu.VMEM((D_MODEL, GATE_PAD), BF16),
        ],
        compiler_params=pltpu.CompilerParams(dimension_semantics=("arbitrary", "arbitrary")),
        name="mixer_prompt",
    )(h, *weights, *[w_all for w_all, _ in cast_jobs])


N_SAMPLE_STATES = 3

def _mixer_sample_kernel(h_ref, s_ref, cbuf_ref, pbuf_ref, win_ref, convw_ref, alog_ref, dtb_ref, onorm_ref,
                         poolw_ref, pscale_ref, wout_ref, g2_ref, b2_ref, *refs):
    refs = refs[N_SAMPLE_STATES:]
    y_ref, snew_ref, cnew_ref, pnew_ref = refs[:4]
    q_scr, k_scr, v_scr, beta_scr, eg_scr, z_scr, opool_scr, o_scr = refs[4:]
    i = pl.program_id(0)
    n_i = pl.num_programs(0)

    @pl.when(i == 0)
    def _():
        proj = jnp.dot(h_ref[...].astype(BF16), win_ref[...], preferred_element_type=F32)
        qkv = proj[:, 0:QKV_DIM]
        conv = convw_ref[CONV_W - 1:CONV_W, :] * qkv
        for j in range(CONV_W - 1):
            conv = conv + convw_ref[j:j + 1, :] * cbuf_ref[j]
        for j in range(CONV_W - 2):
            cnew_ref[j] = cbuf_ref[j + 1]
        cnew_ref[CONV_W - 2] = qkv
        q, k, v = _qkv_heads(conv)
        q_scr[...] = q
        k_scr[...] = k
        v_scr[...] = v
        ba = jnp.concatenate([proj[:, GATE_OFF:GATE_OFF + N_GATE],
                              jnp.zeros((proj.shape[0], GATE_PAD - N_GATE), F32)], axis=-1)
        beta_t, g_t = _gates(ba, alog_ref[...], dtb_ref[...])
        beta_scr[...] = beta_t
        eg_scr[...] = jnp.exp(g_t)
        z_scr[...] = proj[:, Z_OFF:Z_OFF + DN_DIM]

        p = proj[:, POOL_OFF:POOL_OFF + POOL_DIM]
        d_groups = []
        for gi, w in enumerate(POOL_WINDOWS):
            lo = gi * POOL_GROUP_DIM
            tok = p[:, lo:lo + POOL_GROUP_DIM]
            s = tok
            for back in range(1, w):
                s = s + pbuf_ref[POOL_BUF - back, :, lo:lo + POOL_GROUP_DIM]
            cnt = float(min(w, PAST_LEN + 1))
            d_groups.append(s / cnt - tok)
        opool_scr[...] = _pool_project(d_groups, poolw_ref, pscale_ref[...])
        for j in range(POOL_BUF - 1):
            pnew_ref[j] = pbuf_ref[j + 1]
        pnew_ref[POOL_BUF - 1] = p

    rows = pl.ds(pl.multiple_of(i * SAMPLE_TILE, SAMPLE_TILE), SAMPLE_TILE)
    q_t, k_t, v_t = q_scr[rows, :], k_scr[rows, :], v_scr[rows, :]
    beta_t, a_t = beta_scr[rows, :], eg_scr[rows, :]
    eye = jnp.where(lax.broadcasted_iota(jnp.int32, (HEAD_DIM, HEAD_DIM), 0)
                    == lax.broadcasted_iota(jnp.int32, (HEAD_DIM, HEAD_DIM), 1), 1.0, 0.0).astype(F32)
    pad_rows = jnp.zeros((F32_SUBLANES - 2, HEAD_DIM), F32)
    o_heads = [[] for _ in range(SAMPLE_TILE)]
    for hd in range(DN_HEADS):
        sl = slice(hd * HEAD_DIM, (hd + 1) * HEAD_DIM)
        for j in range(SAMPLE_TILE):
            qh = q_t[j:j + 1, sl]
            kh = k_t[j:j + 1, sl]
            vh = v_t[j:j + 1, sl]
            beta = beta_t[j:j + 1, hd:hd + 1]
            a = a_t[j:j + 1, DECAY_LANE + hd:DECAY_LANE + hd + 1]
            s_old = s_ref[j, hd]
            kq = jnp.concatenate([kh, qh, pad_rows], axis=0)
            kq_hi, kq_lo = _split_bf16(kq)
            s_hi, s_lo = _split_bf16(s_old)
            mv = functools.partial(jnp.dot, preferred_element_type=F32)
            kqs = mv(kq_hi, s_hi) + (mv(kq_hi, s_lo) + mv(kq_lo, s_hi))
            ks, qs = kqs[0:1], kqs[1:2]
            k_col = jnp.sum(eye * kh, axis=-1, keepdims=True)
            u = beta * vh - (beta * a) * ks
            qk = jnp.sum(qh * kh, axis=-1, keepdims=True)
            o_heads[j].append(a * qs + qk * u)
            snew_ref[j, hd] = a * s_old + k_col * u
    o_scr[rows, :] = jnp.concatenate([jnp.concatenate(o_h, axis=-1) for o_h in o_heads], axis=0)

    @pl.when(i == n_i - 1)
    def _():
        o_dn = _gated_out_norm(o_scr[...], z_scr[...], onorm_ref[...])
        mix = _dot(jnp.concatenate([o_dn, opool_scr[...]], axis=-1), wout_ref[...])
        y_ref[...] = _layer_norm(DN_ALPHA * h_ref[...] + mix, g2_ref[...], b2_ref[...])


def _mixer_sample(layer, h, s_all, cbuf_all, pbuf_all, new_states, *weights):
    bsz = h.shape[0]
    assert bsz % SAMPLE_TILE == 0 and len(new_states) == N_SAMPLE_STATES
    cshape = (CONV_W - 1, bsz, QKV_DIM)
    pshape = (POOL_BUF, bsz, POOL_DIM)
    assert cbuf_all.shape[1:] == cshape and pbuf_all.shape[1:] == pshape
    out_shape = (
        jax.ShapeDtypeStruct((bsz, D_MODEL), F32),
        jax.ShapeDtypeStruct(s_all.shape, F32),
        jax.ShapeDtypeStruct(cbuf_all.shape, F32),
        jax.ShapeDtypeStruct(pbuf_all.shape, F32),
    )
    state_spec = pl.BlockSpec((None, SAMPLE_TILE, DN_HEADS, HEAD_DIM, HEAD_DIM), lambda i: (layer, i, 0, 0, 0))
    n_in = 4 + 10
    return pl.pallas_call(
        _mixer_sample_kernel,
        out_shape=out_shape,
        grid=(bsz // SAMPLE_TILE,),
        in_specs=[
            _resident((bsz, D_MODEL)),
            state_spec,
            _layer_resident(layer, cshape),
            _layer_resident(layer, pshape),
        ] + _mixer_weight_specs(layer) + [pl.BlockSpec(memory_space=pl.ANY)] * len(new_states),
        out_specs=(
            pl.BlockSpec((bsz, D_MODEL), lambda i: (0, 0)),
            state_spec,
            pl.BlockSpec((None,) + cshape, lambda i: (layer, 0, 0, 0)),
            pl.BlockSpec((None,) + pshape, lambda i: (layer, 0, 0, 0)),
        ),
        input_output_aliases={n_in + k: 1 + k for k in range(len(new_states))},
        scratch_shapes=[
            pltpu.VMEM((bsz, DN_DIM), F32),
            pltpu.VMEM((bsz, DN_DIM), F32),
            pltpu.VMEM((bsz, DN_DIM), F32),
            pltpu.VMEM((bsz, GATE_PAD), F32),
            pltpu.VMEM((bsz, GATE_PAD), F32),
            pltpu.VMEM((bsz, DN_DIM), F32),
            pltpu.VMEM((bsz, POOL_DIM), F32),
            pltpu.VMEM((bsz, DN_DIM), F32),
        ],
        compiler_params=pltpu.CompilerParams(dimension_semantics=("arbitrary",)),
        name="mixer_sample",
    )(h, s_all, cbuf_all, pbuf_all, *weights, *new_states)


def _pad_gate_param(v):
    out = jnp.zeros((v.shape[0], 1, GATE_PAD), F32)
    return out.at[:, 0, DECAY_LANE:DECAY_LANE + DN_HEADS].set(v.astype(F32))


def kernel(x_prompt, x_sample, state_delta, state_conv, state_pool, ln1_g, ln1_b, ffn1_w_gate, ffn1_w_up, ffn1_w_down, w_in, conv_w, a_log, dt_bias, onorm_g, pool_w, pool_scale, w_out, ln2_g, ln2_b, ffn2_w_gate, ffn2_w_up, ffn2_w_down, ln3_g, ln3_b):
    bsz, seq, _ = x_prompt.shape
    dec_b = x_sample.shape[0]

    win = w_in.astype(BF16)
    wout = w_out.astype(BF16)
    poolw = pool_w.astype(BF16)
    ffn1_stacks = (ffn1_w_gate, ffn1_w_up, ffn1_w_down)
    ffn2_stacks = (ffn2_w_gate, ffn2_w_up, ffn2_w_down)
    f1 = tuple(w[0].astype(BF16) for w in ffn1_stacks)
    alog = _pad_gate_param(a_log)
    dtb = _pad_gate_param(dt_bias)
    row = lambda v: v.reshape(DEPTH, 1, -1)
    ln1g, ln1b, ln2g, ln2b, ln3g, ln3b = map(row, (ln1_g, ln1_b, ln2_g, ln2_b, ln3_g, ln3_b))
    onorm = row(onorm_g)
    pscale = row(pool_scale)
    mix_w = (win, conv_w, alog, dtb, onorm, poolw, pscale, wout, ln2g, ln2b)

    xp = x_prompt.reshape(bsz * seq, D_MODEL)
    xs = x_sample.reshape(dec_b, D_MODEL)
    dp, cp, pp = [], [], []
    cbuf_all = jnp.transpose(state_conv, (0, 2, 1, 3))
    pbuf_all = jnp.transpose(state_pool, (0, 2, 1, 3))
    state_shapes = [state_delta.shape, cbuf_all.shape, pbuf_all.shape]
    flat_shapes = [(DEPTH * dec_b,) + state_delta.shape[2:],
                   (DEPTH * (CONV_W - 1) * dec_b, QKV_DIM), (DEPTH * POOL_BUF * dec_b, POOL_DIM)]
    for l in range(DEPTH):
        hp, hs = _ffn_ln(l, xp, xs, *f1, ln1g, ln1b)
        cast_jobs = [(w, l) for w in ffn2_stacks]
        if l + 1 < DEPTH:
            cast_jobs += [(w, l + 1) for w in ffn1_stacks]
        hp, s_p, c_p, p_p, *extra = _mixer_prompt(l, hp.reshape(bsz, seq, D_MODEL), mix_w, cast_jobs,
                                                  flat_shapes if l == 0 else ())
        cast = extra[:len(cast_jobs)]
        if l == 0:
            sample_states = tuple(z.reshape(s) for z, s in zip(extra[len(cast_jobs):], state_shapes))
        f2, f1 = tuple(cast[:3]), tuple(cast[3:])
        hs, *sample_states = _mixer_sample(l, hs, state_delta, cbuf_all, pbuf_all, tuple(sample_states), *mix_w)
        xp, xs = _ffn_ln(l, hp.reshape(bsz * seq, D_MODEL), hs, *f2, ln3g, ln3b)
        dp.append(s_p); cp.append(c_p); pp.append(p_p)
    delta_sample, conv_sample, pool_sample = sample_states
    return (xp.reshape(bsz, seq, D_MODEL), xs.reshape(dec_b, 1, D_MODEL),
            jnp.stack(dp), jnp.stack(cp), jnp.stack(pp),
            delta_sample, jnp.transpose(conv_sample, (0, 2, 1, 3)), jnp.transpose(pool_sample, (0, 2, 1, 3)))
```
